```python
import jax, jax.numpy as jnp
from jax import lax
import numpy as np

D_MODEL = 2048
BATCH = 1
SEQ = 8192
DEPTH = 4

D_MIX = D_MODEL
POOL_WIDTH = D_MIX // 4
POOL_WINDOWS = (2, 4, 8, 16)
POOL_GROUP = POOL_WIDTH // len(POOL_WINDOWS)
CONV_WIDTH = D_MIX // 4
CONV_KERNEL = 31
CONV_PAD = CONV_KERNEL // 2
LRU_WIDTH = D_MIX // 2
LRU_HEADS = 8
LRU_HEAD_DIM = LRU_WIDTH // LRU_HEADS
LRU_CONV = 4
LRU_C = 8.0
D_IN = POOL_WIDTH + 2 * CONV_WIDTH + 2 * LRU_WIDTH
D_FF = 5632
RMS_EPS = 1e-6
LN_EPS = 1e-5

kernel_name = "bidir_hybrid_pool_conv_rglru_macaron"


def rmsnorm(x, g):
    xf = x.astype(jnp.float32)
    y = xf * lax.rsqrt(jnp.mean(xf * xf, axis=-1, keepdims=True) + RMS_EPS)
    return (y * g.astype(jnp.float32)).astype(x.dtype)


def swiglu(x, w_gate, w_up, w_down):
    return (jax.nn.silu(x @ w_gate) * (x @ w_up)) @ w_down


def depthwise_conv(x, w, pad_left, pad_right):
    c = x.shape[-1]
    return lax.conv_general_dilated(
        x, w[:, None, :].astype(x.dtype), window_strides=(1,),
        padding=[(pad_left, pad_right)],
        dimension_numbers=("NWC", "WIO", "NWC"), feature_group_count=c)


def pool_mixer(u, w, scale):
    b, s, _ = u.shape
    uf = u.astype(jnp.float32)
    cs = jnp.pad(jnp.cumsum(uf, axis=1), ((0, 0), (1, 0), (0, 0)))
    t = jnp.arange(s)
    outs = []
    for g, win in enumerate(POOL_WINDOWS):
        lo = jnp.clip(t - win // 2, 0, s)
        hi = jnp.clip(t + win // 2, 0, s)
        csg = cs[..., g * POOL_GROUP:(g + 1) * POOL_GROUP]
        ssum = jnp.take(csg, hi, axis=1) - jnp.take(csg, lo, axis=1)
        cnt = (hi - lo).astype(jnp.float32)[None, :, None]
        outs.append(ssum / cnt)
    pooled = (jnp.concatenate(outs, axis=-1) - uf).astype(u.dtype)
    pooled = pooled.reshape(b, s, len(POOL_WINDOWS), POOL_GROUP)
    mixed = jnp.einsum("bsgi,gij->bsgj", pooled, w).reshape(b, s, POOL_WIDTH)
    return mixed * scale


def conv_module(u, dw_w, dw_b, ln_g, ln_b):
    v, gate = jnp.split(u, 2, axis=-1)
    z = v * jax.nn.sigmoid(gate)
    z = depthwise_conv(z, dw_w, CONV_PAD, CONV_PAD) + dw_b
    zf = z.astype(jnp.float32)
    mu = jnp.mean(zf, axis=-1, keepdims=True)
    var = jnp.mean(jnp.square(zf - mu), axis=-1, keepdims=True)
    zn = ((zf - mu) * lax.rsqrt(var + LN_EPS)).astype(z.dtype) * ln_g + ln_b
    return jax.nn.silu(zn)


def _linear_recurrence(c1, c2):
    a1, b1 = c1
    a2, b2 = c2
    return a1 * a2, a2 * b1 + b2


def rglru_direction(x, conv_w, conv_b, w_a, b_a, w_x, b_x, lam):
    b, s, c = x.shape
    xc = depthwise_conv(x, conv_w, LRU_CONV - 1, 0) + conv_b
    xh = xc.reshape(b, s, LRU_HEADS, LRU_HEAD_DIM)
    r = jax.nn.sigmoid(jnp.einsum("bshi,hij->bshj", xh, w_a).reshape(b, s, c) + b_a)
    i = jax.nn.sigmoid(jnp.einsum("bshi,hij->bshj", xh, w_x).reshape(b, s, c) + b_x)
    log_a = -LRU_C * r.astype(jnp.float32) * jax.nn.softplus(-lam.astype(jnp.float32))
    a = jnp.exp(log_a)
    mult = jnp.sqrt(-jnp.expm1(2.0 * log_a))
    bterm = mult * (i * xc).astype(jnp.float32)
    _, h = lax.associative_scan(_linear_recurrence, (a, bterm), axis=1)
    return h.astype(x.dtype)


def rglru_mixer(u, conv_w, conv_b, w_a, b_a, w_x, b_x, lam):
    gate = jax.nn.gelu(u[..., :LRU_WIDTH])
    xr = u[..., LRU_WIDTH:]
    h_fwd = rglru_direction(xr, conv_w[0], conv_b[0], w_a[0], b_a[0], w_x[0], b_x[0], lam[0])
    h_bwd = rglru_direction(xr[:, ::-1], conv_w[1], conv_b[1], w_a[1], b_a[1], w_x[1], b_x[1], lam[1])[:, ::-1]
    return (h_fwd + h_bwd) * gate


def setup_inputs(seed: int = 0) -> dict:
    key = jax.random.key(seed)
    ks = jax.random.split(key, 32)
    L = DEPTH
    f32 = jnp.float32

    def nrm(k, shape, fan_in):
        return jax.random.normal(k, shape, f32) * (fan_in ** -0.5)

    def gain(k, shape):
        return 1.0 + 0.1 * jax.random.normal(k, shape, f32)

    def bias(k, shape):
        return 0.02 * jax.random.normal(k, shape, f32)

    a8 = jax.random.uniform(ks[31], (L, 2, LRU_WIDTH), f32, minval=0.9, maxval=0.999)
    a_base = a8 ** (1.0 / LRU_C)
    lru_lambda = jnp.log(a_base) - jnp.log1p(-a_base)

    return {
        "x": jax.random.normal(ks[0], (BATCH, SEQ, D_MODEL), f32),
        "norm_ffn1": gain(ks[1], (L, D_MODEL)),
        "ffn1_w_gate": nrm(ks[2], (L, D_MODEL, D_FF), D_MODEL),
        "ffn1_w_up": nrm(ks[3], (L, D_MODEL, D_FF), D_MODEL),
        "ffn1_w_down": nrm(ks[4], (L, D_FF, D_MODEL), D_FF),
        "norm_mix": gain(ks[5], (L, D_MODEL)),
        "w_in": nrm(ks[6], (L, D_MODEL, D_IN), D_MODEL),
        "pool_w": nrm(ks[7], (L, len(POOL_WINDOWS), POOL_GROUP, POOL_GROUP), POOL_GROUP),
        "pool_scale": gain(ks[8], (L, POOL_WIDTH)),
        "conv_dw_w": nrm(ks[9], (L, CONV_KERNEL, CONV_WIDTH), CONV_KERNEL),
        "conv_dw_b": bias(ks[10], (L, CONV_WIDTH)),
        "conv_ln_g": gain(ks[11], (L, CONV_WIDTH)),
        "conv_ln_b": bias(ks[12], (L, CONV_WIDTH)),
        "lru_conv_w": nrm(ks[13], (L, 2, LRU_CONV, LRU_WIDTH), LRU_CONV),
        "lru_conv_b": bias(ks[14], (L, 2, LRU_WIDTH)),
        "lru_w_a": nrm(ks[15], (L, 2, LRU_HEADS, LRU_HEAD_DIM, LRU_HEAD_DIM), LRU_HEAD_DIM),
        "lru_b_a": bias(ks[16], (L, 2, LRU_WIDTH)),
        "lru_w_x": nrm(ks[17], (L, 2, LRU_HEADS, LRU_HEAD_DIM, LRU_HEAD_DIM), LRU_HEAD_DIM),
        "lru_b_x": bias(ks[18], (L, 2, LRU_WIDTH)),
        "lru_lambda": lru_lambda,
        "w_out": nrm(ks[19], (L, D_MIX, D_MODEL), D_MIX),
        "norm_ffn2": gain(ks[20], (L, D_MODEL)),
        "ffn2_w_gate": nrm(ks[21], (L, D_MODEL, D_FF), D_MODEL),
        "ffn2_w_up": nrm(ks[22], (L, D_MODEL, D_FF), D_MODEL),
        "ffn2_w_down": nrm(ks[23], (L, D_FF, D_MODEL), D_FF),
        "norm_final": gain(ks[24], (D_MODEL,)),
    }


def reference(x, norm_ffn1, ffn1_w_gate, ffn1_w_up, ffn1_w_down, norm_mix, w_in,
              pool_w, pool_scale, conv_dw_w, conv_dw_b, conv_ln_g, conv_ln_b,
              lru_conv_w, lru_conv_b, lru_w_a, lru_b_a, lru_w_x, lru_b_x, lru_lambda,
              w_out, norm_ffn2, ffn2_w_gate, ffn2_w_up, ffn2_w_down, norm_final):
    split_conv = POOL_WIDTH
    split_lru = POOL_WIDTH + 2 * CONV_WIDTH
    for l in range(DEPTH):
        h = rmsnorm(x, norm_ffn1[l])
        x = x + 0.5 * swiglu(h, ffn1_w_gate[l], ffn1_w_up[l], ffn1_w_down[l])

        h = rmsnorm(x, norm_mix[l])
        u = h @ w_in[l]
        u_pool = u[..., :split_conv]
        u_conv = u[..., split_conv:split_lru]
        u_lru = u[..., split_lru:]
        y_pool = pool_mixer(u_pool, pool_w[l], pool_scale[l])
        y_conv = conv_module(u_conv, conv_dw_w[l], conv_dw_b[l], conv_ln_g[l], conv_ln_b[l])
        y_lru = rglru_mixer(u_lru, lru_conv_w[l], lru_conv_b[l], lru_w_a[l], lru_b_a[l],
                            lru_w_x[l], lru_b_x[l], lru_lambda[l])
        y = jnp.concatenate([y_pool, y_conv, y_lru], axis=-1)
        x = x + y @ w_out[l]

        h = rmsnorm(x, norm_ffn2[l])
        x = x + 0.5 * swiglu(h, ffn2_w_gate[l], ffn2_w_up[l], ffn2_w_down[l])
    return rmsnorm(x, norm_final)
```

```python
import functools

import jax
import jax.numpy as jnp
from jax import lax
from jax.experimental import pallas as pl
from jax.experimental.pallas import tpu as pltpu

RMS_EPS = 1e-6
LN_EPS = 1e-5
LRU_C = 8.0
POOL_WINDOWS = (2, 4, 8, 16)

SUBLANES = 8
POOL_HALO = 8
CONV_HALO = 16
V7X_VMEM_LIMIT_BYTES = 56 * 1024 * 1024

F32 = jnp.float32
BF16 = jnp.bfloat16

FFN_ROWS = 1024
FFN_COLS = 512
MIX_ROWS = 256


def _rms(x, g):
    return x * lax.rsqrt(jnp.mean(x * x, axis=-1, keepdims=True) + RMS_EPS) * g


def _resident(shape):
    zeros = (0,) * len(shape)
    return pl.BlockSpec(shape, lambda *_: zeros, pipeline_mode=pl.Buffered(1))


def _layer_block(shape, layer):
    zeros = (0,) * len(shape)
    return pl.BlockSpec((None,) + tuple(shape), lambda *_: (layer,) + zeros, pipeline_mode=pl.Buffered(1))


def _ffn_body(x_ref, g_ref, wg_ref, wu_ref, wd_ref, gf_ref, o_ref, h_ref, *, n_ff, d_chunk, final_norm):
    j = pl.program_id(1)

    @pl.when(j == 0)
    def _():
        x = x_ref[...]
        h_ref[...] = _rms(x, g_ref[...]).astype(BF16)
        o_ref[...] = x

    h = h_ref[...]
    gate = jnp.dot(h, wg_ref[...].astype(BF16), preferred_element_type=F32)
    up = jnp.dot(h, wu_ref[...].astype(BF16), preferred_element_type=F32)
    act = ((gate * jax.nn.sigmoid(gate)) * (0.5 * up)).astype(BF16)
    d = o_ref.shape[1]
    for c in range(0, d, d_chunk):
        o_ref[:, c:c + d_chunk] += jnp.dot(act, wd_ref[:, c:c + d_chunk].astype(BF16),
                                           preferred_element_type=F32)

    if final_norm:
        @pl.when(j == n_ff - 1)
        def _():
            o_ref[...] = _rms(o_ref[...], gf_ref[...])


def _ffn(x, norm_g, w_gate, w_up, w_down, norm_final, layer, *, final_norm, tm, tf):
    s, d = x.shape
    f = w_gate.shape[-1]
    tm = min(tm, s)
    tf = min(tf, f)
    n_ff = f // tf
    body = functools.partial(_ffn_body, n_ff=n_ff, d_chunk=min(512, d), final_norm=final_norm)
    return pl.pallas_call(
        body,
        grid=(s // tm, n_ff),
        in_specs=[
            pl.BlockSpec((tm, d), lambda i, j: (i, 0), pipeline_mode=pl.Buffered(1)),
            pl.BlockSpec((None, 1, d), lambda i, j: (layer, 0, 0)),
            pl.BlockSpec((None, d, tf), lambda i, j: (layer, 0, j)),
            pl.BlockSpec((None, d, tf), lambda i, j: (layer, 0, j)),
            pl.BlockSpec((None, tf, d), lambda i, j: (layer, j, 0)),
            pl.BlockSpec((1, d), lambda i, j: (0, 0)),
        ],
        out_specs=pl.BlockSpec((tm, d), lambda i, j: (i, 0), pipeline_mode=pl.Buffered(1)),
        out_shape=jax.ShapeDtypeStruct((s, d), F32),
        scratch_shapes=[pltpu.VMEM((tm, d), BF16)],
        compiler_params=pltpu.CompilerParams(
            dimension_semantics=("arbitrary", "arbitrary"),
            vmem_limit_bytes=V7X_VMEM_LIMIT_BYTES),
        name="ffn",
    )(x, norm_g, w_gate, w_up, w_down, norm_final)


def _proj_body(x_ref, g_ref, w_ref, ua_ref, xr_ref):
    h = _rms(x_ref[...], g_ref[...]).astype(BF16)
    na = ua_ref.shape[1]
    ua_ref[...] = jnp.dot(h, w_ref[:, :na], preferred_element_type=F32)
    xr_ref[...] = jnp.dot(h, w_ref[:, na:], preferred_element_type=F32)


def _proj(x, norm_g, w_in_bf16, layer, n_rec, *, tm):
    s, d = x.shape
    d_in = w_in_bf16.shape[-1]
    na = d_in - n_rec
    tm = min(tm, s)
    return pl.pallas_call(
        _proj_body,
        grid=(s // tm,),
        in_specs=[
            pl.BlockSpec((tm, d), lambda i: (i, 0)),
            _layer_block((1, d), layer),
            _layer_block((d, d_in), layer),
        ],
        out_specs=[pl.BlockSpec((tm, na), lambda i: (i, 0)),
                   pl.BlockSpec((tm, n_rec), lambda i: (i, 0))],
        out_shape=[jax.ShapeDtypeStruct((s, na), F32), jax.ShapeDtypeStruct((s, n_rec), F32)],
        compiler_params=pltpu.CompilerParams(
            dimension_semantics=("arbitrary",), vmem_limit_bytes=V7X_VMEM_LIMIT_BYTES),
        name="proj",
    )(x, norm_g, w_in_bf16)


def _lru_coeffs(ext_ref, offsets, cw_ref, cb_ref, wax_ref, bax_ref, lam_ref, a_ref, b_ref):
    tm, c = a_ref.shape
    n_heads = wax_ref.shape[0]
    hd = c // n_heads
    xc = cb_ref[...] + cw_ref[0:1, :] * ext_ref[pl.ds(offsets[0], tm), :]
    for k in range(1, len(offsets)):
        xc = xc + cw_ref[k:k + 1, :] * ext_ref[pl.ds(offsets[k], tm), :]
    softplus_neg_lam = jax.nn.softplus(-lam_ref[...])
    for h in range(n_heads):
        sl = slice(h * hd, (h + 1) * hd)
        xh = xc[:, sl]
        ri = jnp.dot(xh.astype(BF16), wax_ref[h], preferred_element_type=F32) + bax_ref[h]
        r = jax.nn.sigmoid(ri[:, :hd])
        ig = jax.nn.sigmoid(ri[:, hd:])
        log_a = -LRU_C * r * softplus_neg_lam[:, sl]
        a = jnp.exp(log_a)
        a_ref[:, sl] = a
        b_ref[:, sl] = jnp.sqrt(1.0 - a * a) * (ig * xh)


def _lru_scan(a_ref, b_ref, h_ref, carry_ref, *, reverse):
    tm, c = a_ref.shape
    n_groups = tm // SUBLANES
    row = lax.broadcasted_iota(jnp.int32, (SUBLANES, c), 0)

    def group(g, h_edge):
        gi = (n_groups - 1 - g) if reverse else g
        sl = pl.ds(pl.multiple_of(gi * SUBLANES, SUBLANES), SUBLANES)
        a = a_ref[sl, :]
        b = b_ref[sl, :]
        for dist in (1, 2, 4):
            if reverse:
                keep = row < SUBLANES - dist
                shift = SUBLANES - dist
            else:
                keep = row >= dist
                shift = dist
            a_n = jnp.where(keep, pltpu.roll(a, shift, 0), 1.0)
            b_n = jnp.where(keep, pltpu.roll(b, shift, 0), 0.0)
            b = a * b_n + b
            a = a * a_n
        h = a * h_edge + b
        h_ref[sl, :] = h
        edge = h[0:1, :] if reverse else h[SUBLANES - 1:SUBLANES, :]
        return jnp.broadcast_to(edge, (SUBLANES, c))

    carry_ref[...] = lax.fori_loop(0, n_groups, group, carry_ref[...])


def _lru_bwd_body(xr_ref, nxt_ref, cw_ref, cb_ref, wax_ref, bax_ref, lam_ref, hb_ref,
                  ext_ref, a_ref, b_ref, carry_ref):
    i = pl.program_id(0)
    tm = xr_ref.shape[0]

    @pl.when(i == 0)
    def _():
        carry_ref[...] = jnp.zeros_like(carry_ref)

    ext_ref[0:tm, :] = xr_ref[...]
    ext_ref[tm:tm + SUBLANES, :] = jnp.where(i == 0, 0.0, nxt_ref[...])
    n_taps = cw_ref.shape[0]
    offsets = tuple(n_taps - 1 - k for k in range(n_taps))
    _lru_coeffs(ext_ref, offsets, cw_ref, cb_ref, wax_ref, bax_ref, lam_ref, a_ref, b_ref)
    _lru_scan(a_ref, b_ref, hb_ref, carry_ref, reverse=True)


def _lru_bwd(xr, cw, cb, wax, bax, lam, layer, *, tm):
    s, c = xr.shape
    tm = min(tm, s)
    nt = s // tm
    n_taps = cw.shape[2]
    n_heads, hd = wax.shape[2], wax.shape[3]
    last_halo = s // SUBLANES - 1
    per_tile = tm // SUBLANES

    def dir_block(shape):
        zeros = (0,) * len(shape)
        return pl.BlockSpec((None, None) + tuple(shape), lambda i: (layer, 1) + zeros,
                            pipeline_mode=pl.Buffered(1))

    return pl.pallas_call(
        _lru_bwd_body,
        grid=(nt,),
        in_specs=[
            pl.BlockSpec((tm, c), lambda i: (nt - 1 - i, 0)),
            pl.BlockSpec((SUBLANES, c), lambda i: (jnp.minimum((nt - i) * per_tile, last_halo), 0)),
            dir_block((n_taps, c)),
            dir_block((1, c)),
            dir_block((n_heads, hd, 2 * hd)),
            dir_block((n_heads, 1, 2 * hd)),
            dir_block((1, c)),
        ],
        out_specs=pl.BlockSpec((tm, c), lambda i: (nt - 1 - i, 0)),
        out_shape=jax.ShapeDtypeStruct((s, c), F32),
        scratch_shapes=[pltpu.VMEM((tm + SUBLANES, c), F32), pltpu.VMEM((tm, c), F32),
                        pltpu.VMEM((tm, c), F32), pltpu.VMEM((SUBLANES, c), F32)],
        compiler_params=pltpu.CompilerParams(
            dimension_semantics=("arbitrary",), vmem_limit_bytes=V7X_VMEM_LIMIT_BYTES),
        name="lru_bwd",
    )(xr, xr, cw, cb, wax, bax, lam)


def _mix_body(x_ref, ua_ref, uprev_ref, unext_ref, xr_ref, xrprev_ref, hb_ref,
              pw_ref, ps_ref, dww_ref, dwb_ref, lng_ref, lnb_ref,
              cw_ref, cb_ref, wax_ref, bax_ref, lam_ref, wout_ref,
              o_ref,
              pext_ref, zext_ref, xext_ref, a_ref, b_ref, hf_ref, y_ref, carry_ref, *, seq_len):
    i = pl.program_id(0)
    nt = pl.num_programs(0)
    tm = x_ref.shape[0]
    first = i == 0
    last = i == nt - 1
    n_groups, pg = pw_ref.shape[0], pw_ref.shape[1]
    pool_w = n_groups * pg
    conv_w = dww_ref.shape[1]
    lru_w = xr_ref.shape[1]
    v0, g0, lg0 = pool_w, pool_w + conv_w, pool_w + 2 * conv_w

    @pl.when(first)
    def _():
        carry_ref[...] = jnp.zeros_like(carry_ref)

    pext_ref[0:POOL_HALO, :] = jnp.where(first, 0.0, uprev_ref[CONV_HALO - POOL_HALO:, 0:pool_w])
    pext_ref[POOL_HALO:POOL_HALO + tm, :] = ua_ref[:, 0:pool_w]
    pext_ref[POOL_HALO + tm:, :] = jnp.where(last, 0.0, unext_ref[0:POOL_HALO, 0:pool_w])
    t = i * tm + lax.broadcasted_iota(jnp.int32, (tm, pg), 0)
    for g, win in enumerate(POOL_WINDOWS):
        sl = slice(g * pg, (g + 1) * pg)
        half = win // 2
        ssum = pext_ref[pl.ds(POOL_HALO - half, tm), sl]
        for o in range(-half + 1, half):
            ssum = ssum + pext_ref[pl.ds(POOL_HALO + o, tm), sl]
        cnt = (jnp.minimum(t + half, seq_len) - jnp.maximum(t - half, 0)).astype(F32)
        pooled = ssum / cnt - ua_ref[:, sl]
        mixed = jnp.dot(pooled.astype(BF16), pw_ref[g], preferred_element_type=F32)
        y_ref[:, sl] = (mixed * ps_ref[:, sl]).astype(BF16)

    def glu(u_ref):
        return u_ref[:, v0:v0 + conv_w] * jax.nn.sigmoid(u_ref[:, g0:g0 + conv_w])

    zext_ref[0:CONV_HALO, :] = jnp.where(first, 0.0, glu(uprev_ref))
    zext_ref[CONV_HALO:CONV_HALO + tm, :] = glu(ua_ref)
    zext_ref[CONV_HALO + tm:, :] = jnp.where(last, 0.0, glu(unext_ref))
    n_taps = dww_ref.shape[0]
    pad = n_taps // 2
    z = dwb_ref[...] + dww_ref[0:1, :] * zext_ref[pl.ds(CONV_HALO - pad, tm), :]
    for k in range(1, n_taps):
        z = z + dww_ref[k:k + 1, :] * zext_ref[pl.ds(CONV_HALO - pad + k, tm), :]
    mu = jnp.mean(z, axis=-1, keepdims=True)
    zc = z - mu
    var = jnp.mean(zc * zc, axis=-1, keepdims=True)
    zn = zc * lax.rsqrt(var + LN_EPS) * lng_ref[...] + lnb_ref[...]
    y_ref[:, pool_w:pool_w + conv_w] = (zn * jax.nn.sigmoid(zn)).astype(BF16)

    xext_ref[0:SUBLANES, :] = jnp.where(first, 0.0, xrprev_ref[...])
    xext_ref[SUBLANES:, :] = xr_ref[...]
    lru_taps = cw_ref.shape[0]
    offsets = tuple(SUBLANES - (lru_taps - 1) + k for k in range(lru_taps))
    _lru_coeffs(xext_ref, offsets, cw_ref, cb_ref, wax_ref, bax_ref, lam_ref, a_ref, b_ref)
    _lru_scan(a_ref, b_ref, hf_ref, carry_ref, reverse=False)
    gate = jax.nn.gelu(ua_ref[:, lg0:lg0 + lru_w])
    y_ref[:, pool_w + conv_w:] = ((hf_ref[...] + hb_ref[...]) * gate).astype(BF16)

    o_ref[...] = x_ref[...] + jnp.dot(y_ref[...], wout_ref[...], preferred_element_type=F32)


def _mix(x, ua, xr, hb, pool_w, pool_scale, dw_w, dw_b, ln_g, ln_b, cw, cb, wax, bax, lam, w_out_bf16,
         layer, *, tm):
    s, d = x.shape
    na = ua.shape[1]
    c = xr.shape[1]
    tm = min(tm, s)
    nt = s // tm
    n_pool_groups, pg = pool_w.shape[1], pool_w.shape[2]
    pool_width = n_pool_groups * pg
    conv_width = dw_w.shape[2]
    conv_taps = dw_w.shape[1]
    lru_taps = cw.shape[2]
    n_heads, hd = wax.shape[2], wax.shape[3]
    d_mix = w_out_bf16.shape[1]
    assert conv_taps // 2 < CONV_HALO and max(POOL_WINDOWS) // 2 <= POOL_HALO and lru_taps - 1 <= SUBLANES
    assert tm % CONV_HALO == 0 and pool_width + conv_width + c == d_mix
    ch_per_tile = tm // CONV_HALO
    sl_per_tile = tm // SUBLANES
    last_ch = s // CONV_HALO - 1

    def dir_block(shape):
        zeros = (0,) * len(shape)
        return pl.BlockSpec((None, None) + tuple(shape), lambda i: (layer, 0) + zeros,
                            pipeline_mode=pl.Buffered(1))

    body = functools.partial(_mix_body, seq_len=s)
    return pl.pallas_call(
        body,
        grid=(nt,),
        in_specs=[
            pl.BlockSpec((tm, d), lambda i: (i, 0)),
            pl.BlockSpec((tm, na), lambda i: (i, 0)),
            pl.BlockSpec((CONV_HALO, na), lambda i: (jnp.maximum(i * ch_per_tile - 1, 0), 0)),
            pl.BlockSpec((CONV_HALO, na), lambda i: (jnp.minimum((i + 1) * ch_per_tile, last_ch), 0)),
            pl.BlockSpec((tm, c), lambda i: (i, 0)),
            pl.BlockSpec((SUBLANES, c), lambda i: (jnp.maximum(i * sl_per_tile - 1, 0), 0)),
            pl.BlockSpec((tm, c), lambda i: (i, 0)),
            _layer_block((n_pool_groups, pg, pg), layer),
            _layer_block((1, pool_width), layer),
            _layer_block((conv_taps, conv_width), layer),
            _layer_block((1, conv_width), layer),
            _layer_block((1, conv_width), layer),
            _layer_block((1, conv_width), layer),
            dir_block((lru_taps, c)),
            dir_block((1, c)),
            dir_block((n_heads, hd, 2 * hd)),
            dir_block((n_heads, 1, 2 * hd)),
            dir_block((1, c)),
            _layer_block((d_mix, d), layer),
        ],
        out_specs=pl.BlockSpec((tm, d), lambda i: (i, 0)),
        out_shape=jax.ShapeDtypeStruct((s, d), F32),
        scratch_shapes=[
            pltpu.VMEM((tm + 2 * POOL_HALO, pool_width), F32),
            pltpu.VMEM((tm + 2 * CONV_HALO, conv_width), F32),
            pltpu.VMEM((tm + SUBLANES, c), F32),
            pltpu.VMEM((tm, c), F32),
            pltpu.VMEM((tm, c), F32),
            pltpu.VMEM((tm, c), F32),
            pltpu.VMEM((tm, d_mix), BF16),
            pltpu.VMEM((SUBLANES, c), F32),
        ],
        compiler_params=pltpu.CompilerParams(
            dimension_semantics=("arbitrary",), vmem_limit_bytes=V7X_VMEM_LIMIT_BYTES),
        name="mix",
    )(x, ua, ua, ua, xr, xr, hb, pool_w, pool_scale, dw_w, dw_b, ln_g, ln_b,
      cw, cb, wax, bax, lam, w_out_bf16)


def kernel(x, norm_ffn1, ffn1_w_gate, ffn1_w_up, ffn1_w_down, norm_mix, w_in, pool_w, pool_scale, conv_dw_w, conv_dw_b, conv_ln_g, conv_ln_b, lru_conv_w, lru_conv_b, lru_w_a, lru_b_a, lru_w_x, lru_b_x, lru_lambda, w_out, norm_ffn2, ffn2_w_gate, ffn2_w_up, ffn2_w_down, norm_final):
    b, s, d = x.shape
    depth = w_in.shape[0]
    n_dirs, n_heads, hd = lru_w_a.shape[1], lru_w_a.shape[2], lru_w_a.shape[3]
    lru_w = n_heads * hd

    row = lambda p: p[:, None, :]
    norm_ffn1_r, norm_mix_r, norm_ffn2_r = row(norm_ffn1), row(norm_mix), row(norm_ffn2)
    norm_final_r = norm_final[None, :]
    pool_scale_r, dw_b_r, ln_g_r, ln_b_r = row(pool_scale), row(conv_dw_b), row(conv_ln_g), row(conv_ln_b)
    lru_cb_r = lru_conv_b[:, :, None, :]
    lam_r = lru_lambda[:, :, None, :]
    wax = jnp.concatenate([lru_w_a, lru_w_x], axis=-1).astype(BF16)
    bax = jnp.concatenate([lru_b_a.reshape(depth, n_dirs, n_heads, 1, hd),
                           lru_b_x.reshape(depth, n_dirs, n_heads, 1, hd)], axis=-1)
    w_in_b = w_in.astype(BF16)
    w_out_b = w_out.astype(BF16)
    pool_w_b = pool_w.astype(BF16)

    outs = []
    for bi in range(b):
        xb = x[bi]
        for l in range(depth):
            xb = _ffn(xb, norm_ffn1_r, ffn1_w_gate, ffn1_w_up, ffn1_w_down, norm_final_r, l,
                      final_norm=False, tm=FFN_ROWS, tf=FFN_COLS)
            ua, xr = _proj(xb, norm_mix_r, w_in_b, l, lru_w, tm=MIX_ROWS)
            hb = _lru_bwd(xr, lru_conv_w, lru_cb_r, wax, bax, lam_r, l, tm=MIX_ROWS)
            xb = _mix(xb, ua, xr, hb, pool_w_b, pool_scale_r, conv_dw_w, dw_b_r, ln_g_r, ln_b_r,
                      lru_conv_w, lru_cb_r, wax, bax, lam_r, w_out_b, l, tm=MIX_ROWS)
            xb = _ffn(xb, norm_ffn2_r, ffn2_w_gate, ffn2_w_up, ffn2_w_down, norm_final_r, l,
                      final_norm=(l == depth - 1), tm=FFN_ROWS, tf=FFN_COLS)
        outs.append(xb)
    return jnp.stack(outs, axis=0) if b > 1 else outs[0][None]
```

```python
import functools

import jax
import jax.numpy as jnp
from jax import lax
from jax.experimental import pallas as pl
from jax.experimental.pallas import tpu as pltpu

RMS_EPS = 1e-6
LN_EPS = 1e-5
LRU_C = 8.0
POOL_WINDOWS = (2, 4, 8, 16)

SUBLANES = 8
POOL_HALO = 8
CONV_HALO = 16
V7X_VMEM_LIMIT_BYTES = 56 * 1024 * 1024

F32 = jnp.float32
BF16 = jnp.bfloat16

FFN_ROWS = 1024
FFN_COLS = 512
MIX_ROWS = 256


def _rms(x, g):
    return x * lax.rsqrt(jnp.mean(x * x, axis=-1, keepdims=True) + RMS_EPS) * g


def _layer_block(shape, layer):
    zeros = (0,) * len(shape)
    return pl.BlockSpec((None,) + tuple(shape), lambda *_: (layer,) + zeros, pipeline_mode=pl.Buffered(1))


def _dir_block(shape, layer, direction):
    zeros = (0,) * len(shape)
    return pl.BlockSpec((None, None) + tuple(shape), lambda *_: (layer, direction) + zeros,
                        pipeline_mode=pl.Buffered(1))


def _ffn_body(x_hbm, g_ref, wg_ref, wu_ref, wd_ref, gf_ref, o_hbm, acc_ref, h_ref, in_sem, out_sem,
              *, n_ff, d_chunk, final_norm):
    i = pl.program_id(0)
    j = pl.program_id(1)
    n_tiles = pl.num_programs(0)
    tm = acc_ref.shape[1]
    slot = i % 2
    other = 1 - slot

    def x_copy(tile, s):
        return pltpu.make_async_copy(x_hbm.at[pl.ds(tile * tm, tm)], acc_ref.at[s], in_sem.at[s])

    def o_copy(tile, s):
        return pltpu.make_async_copy(acc_ref.at[s], o_hbm.at[pl.ds(tile * tm, tm)], out_sem.at[s])

    @pl.when(j == 0)
    def _():
        @pl.when(i == 0)
        def _():
            x_copy(0, 0).start()

        x_copy(i, slot).wait()
        h_ref[...] = _rms(acc_ref[slot], g_ref[...]).astype(BF16)

    @pl.when(j == 1)
    def _():
        @pl.when(i >= 1)
        def _():
            o_copy(i - 1, other).wait()

        @pl.when(i + 1 < n_tiles)
        def _():
            x_copy(i + 1, other).start()

    h = h_ref[...]
    gate = jnp.dot(h, wg_ref[...].astype(BF16), preferred_element_type=F32)
    up = jnp.dot(h, wu_ref[...].astype(BF16), preferred_element_type=F32)
    act = ((gate * jax.nn.sigmoid(gate)) * (0.5 * up)).astype(BF16)
    d = acc_ref.shape[2]
    for c in range(0, d, d_chunk):
        acc_ref[slot, :, c:c + d_chunk] += jnp.dot(act, wd_ref[:, c:c + d_chunk].astype(BF16),
                                                   preferred_element_type=F32)

    @pl.when(j == n_ff - 1)
    def _():
        if final_norm:
            acc_ref[slot] = _rms(acc_ref[slot], gf_ref[...])
        o_copy(i, slot).start()

        @pl.when(i == n_tiles - 1)
        def _():
            o_copy(i, slot).wait()


def _ffn(x, norm_g, w_gate, w_up, w_down, norm_final, layer, *, final_norm, tm, tf):
    s, d = x.shape
    f = w_gate.shape[-1]
    tm = min(tm, s)
    tf = min(tf, f)
    n_ff = f // tf
    assert n_ff >= 2
    body = functools.partial(_ffn_body, n_ff=n_ff, d_chunk=min(512, d), final_norm=final_norm)
    return pl.pallas_call(
        body,
        grid=(s // tm, n_ff),
        in_specs=[
            pl.BlockSpec(memory_space=pl.ANY),
            pl.BlockSpec((None, 1, d), lambda i, j: (layer, 0, 0)),
            pl.BlockSpec((None, d, tf), lambda i, j: (layer, 0, j)),
            pl.BlockSpec((None, d, tf), lambda i, j: (layer, 0, j)),
            pl.BlockSpec((None, tf, d), lambda i, j: (layer, j, 0)),
            pl.BlockSpec((1, d), lambda i, j: (0, 0)),
        ],
        out_specs=pl.BlockSpec(memory_space=pl.ANY),
        out_shape=jax.ShapeDtypeStruct((s, d), F32),
        scratch_shapes=[pltpu.VMEM((2, tm, d), F32), pltpu.VMEM((tm, d), BF16),
                        pltpu.SemaphoreType.DMA((2,)), pltpu.SemaphoreType.DMA((2,))],
        compiler_params=pltpu.CompilerParams(
            dimension_semantics=("arbitrary", "arbitrary"),
            vmem_limit_bytes=V7X_VMEM_LIMIT_BYTES),
        name="ffn",
    )(x, norm_g, w_gate, w_up, w_down, norm_final)


def _lru_coeffs(ext_ref, offsets, cw_ref, cb_ref, wax_ref, bax_ref, lam_ref, a_ref, b_ref):
    tm, c = a_ref.shape
    n_heads = wax_ref.shape[0]
    hd = c // n_heads
    xc = cb_ref[...] + cw_ref[0:1, :] * ext_ref[pl.ds(offsets[0], tm), :]
    for k in range(1, len(offsets)):
        xc = xc + cw_ref[k:k + 1, :] * ext_ref[pl.ds(offsets[k], tm), :]
    softplus_neg_lam = jax.nn.softplus(-lam_ref[...])
    for h in range(n_heads):
        sl = slice(h * hd, (h + 1) * hd)
        xh = xc[:, sl]
        ri = jnp.dot(xh.astype(BF16), wax_ref[h], preferred_element_type=F32) + bax_ref[h]
        r = jax.nn.sigmoid(ri[:, :hd])
        ig = jax.nn.sigmoid(ri[:, hd:])
        log_a = -LRU_C * r * softplus_neg_lam[:, sl]
        a = jnp.exp(log_a)
        a_ref[:, sl] = a
        b_ref[:, sl] = jnp.sqrt(1.0 - a * a) * (ig * xh)


def _lru_scan(a_ref, b_ref, h_ref, carry_ref, *, reverse):
    tm, c = a_ref.shape
    n_groups = tm // SUBLANES
    row = lax.broadcasted_iota(jnp.int32, (SUBLANES, c), 0)
    h_edge = carry_ref[...]
    for g in range(n_groups):
        gi = (n_groups - 1 - g) if reverse else g
        sl = slice(gi * SUBLANES, (gi + 1) * SUBLANES)
        a = a_ref[sl, :]
        b = b_ref[sl, :]
        for dist in (1, 2, 4):
            if reverse:
                keep = row < SUBLANES - dist
                shift = SUBLANES - dist
            else:
                keep = row >= dist
                shift = dist
            a_n = jnp.where(keep, pltpu.roll(a, shift, 0), 1.0)
            b_n = jnp.where(keep, pltpu.roll(b, shift, 0), 0.0)
            b = a * b_n + b
            a = a * a_n
        h = a * h_edge + b
        h_ref[sl, :] = h
        edge = h[0:1, :] if reverse else h[SUBLANES - 1:SUBLANES, :]
        h_edge = jnp.broadcast_to(edge, (SUBLANES, c))
    carry_ref[...] = h_edge


def _proj_bwd_body(x_ref, g_ref, w_ref, cw_ref, cb_ref, wax_ref, bax_ref, lam_ref,
                   ua_ref, xr_ref, hb_ref, ext_ref, a_ref, b_ref, carry_ref):
    i = pl.program_id(0)
    tm = x_ref.shape[0]
    na = ua_ref.shape[1]

    @pl.when(i == 0)
    def _():
        carry_ref[...] = jnp.zeros_like(carry_ref)
        ext_ref[0:SUBLANES, :] = jnp.zeros((SUBLANES, ext_ref.shape[1]), F32)

    h = _rms(x_ref[...], g_ref[...]).astype(BF16)
    xr = jnp.dot(h, w_ref[:, na:], preferred_element_type=F32)
    ua_ref[...] = jnp.dot(h, w_ref[:, :na], preferred_element_type=F32)
    xr_ref[...] = xr
    ext_ref[tm:tm + SUBLANES, :] = ext_ref[0:SUBLANES, :]
    ext_ref[0:tm, :] = xr
    n_taps = cw_ref.shape[0]
    offsets = tuple(n_taps - 1 - k for k in range(n_taps))
    _lru_coeffs(ext_ref, offsets, cw_ref, cb_ref, wax_ref, bax_ref, lam_ref, a_ref, b_ref)
    _lru_scan(a_ref, b_ref, hb_ref, carry_ref, reverse=True)


def _proj_bwd(x, norm_g, w_in_bf16, cw, cb, wax, bax, lam, layer, *, tm):
    s, d = x.shape
    d_in = w_in_bf16.shape[-1]
    n_taps = cw.shape[2]
    n_heads, hd = wax.shape[2], wax.shape[3]
    c = n_heads * hd
    na = d_in - c
    tm = min(tm, s)
    nt = s // tm
    assert n_taps - 1 <= SUBLANES
    rev = lambda i: (nt - 1 - i, 0)
    return pl.pallas_call(
        _proj_bwd_body,
        grid=(nt,),
        in_specs=[
            pl.BlockSpec((tm, d), rev),
            _layer_block((1, d), layer),
            _layer_block((d, d_in), layer),
            _dir_block((n_taps, c), layer, 1),
            _dir_block((1, c), layer, 1),
            _dir_block((n_heads, hd, 2 * hd), layer, 1),
            _dir_block((n_heads, 1, 2 * hd), layer, 1),
            _dir_block((1, c), layer, 1),
        ],
        out_specs=[pl.BlockSpec((tm, na), rev), pl.BlockSpec((tm, c), rev), pl.BlockSpec((tm, c), rev)],
        out_shape=[jax.ShapeDtypeStruct((s, na), F32), jax.ShapeDtypeStruct((s, c), F32),
                   jax.ShapeDtypeStruct((s, c), F32)],
        scratch_shapes=[pltpu.VMEM((tm + SUBLANES, c), F32), pltpu.VMEM((tm, c), F32),
                        pltpu.VMEM((tm, c), F32), pltpu.VMEM((SUBLANES, c), F32)],
        compiler_params=pltpu.CompilerParams(
            dimension_semantics=("arbitrary",), vmem_limit_bytes=V7X_VMEM_LIMIT_BYTES),
        name="proj_bwd",
    )(x, norm_g, w_in_bf16, cw, cb, wax, bax, lam)


def _mix_step(y_prev_ref, y_ref, x_ref, ua_ref, uprev_ref, unext_ref, xr_ref, xrprev_ref, hb_ref,
              pw_ref, ps_ref, dww_ref, dwb_ref, lng_ref, lnb_ref,
              cw_ref, cb_ref, wax_ref, bax_ref, lam_ref, wout_ref, o_ref,
              pext_ref, zext_ref, zsh_ref, xext_ref, a_ref, b_ref, hf_ref, carry_ref, *, seq_len, n_tiles):
    step = pl.program_id(0)
    i = jnp.minimum(step, n_tiles - 1)
    tm = ua_ref.shape[0]
    first = i == 0
    last = i == n_tiles - 1
    n_groups, pg = pw_ref.shape[0], pw_ref.shape[1]
    pool_w = n_groups * pg
    conv_w = dww_ref.shape[1]
    lru_w = xr_ref.shape[1]
    v0, g0, lg0 = pool_w, pool_w + conv_w, pool_w + 2 * conv_w

    o_ref[...] = x_ref[...] + jnp.dot(y_prev_ref[...], wout_ref[...], preferred_element_type=F32)

    pext_ref[0:POOL_HALO, :] = jnp.where(first, 0.0, uprev_ref[CONV_HALO - POOL_HALO:, 0:pool_w])
    pext_ref[POOL_HALO:POOL_HALO + tm, :] = ua_ref[:, 0:pool_w]
    pext_ref[POOL_HALO + tm:, :] = jnp.where(last, 0.0, unext_ref[0:POOL_HALO, 0:pool_w])
    t = i * tm + lax.broadcasted_iota(jnp.int32, (tm, pg), 0)
    for g, win in enumerate(POOL_WINDOWS):
        sl = slice(g * pg, (g + 1) * pg)
        half = win // 2
        ssum = pext_ref[pl.ds(POOL_HALO - half, tm), sl]
        for o in range(-half + 1, half):
            ssum = ssum + pext_ref[pl.ds(POOL_HALO + o, tm), sl]
        cnt = (jnp.minimum(t + half, seq_len) - jnp.maximum(t - half, 0)).astype(F32)
        pooled = ssum / cnt - ua_ref[:, sl]
        mixed = jnp.dot(pooled.astype(BF16), pw_ref[g], preferred_element_type=F32)
        y_ref[:, sl] = (mixed * ps_ref[:, sl]).astype(BF16)

    def glu(u_ref):
        return u_ref[:, v0:v0 + conv_w] * jax.nn.sigmoid(u_ref[:, g0:g0 + conv_w])

    zext_ref[0:CONV_HALO, :] = jnp.where(first, 0.0, glu(uprev_ref))
    zext_ref[CONV_HALO:CONV_HALO + tm, :] = glu(ua_ref)
    zext_ref[CONV_HALO + tm:, :] = jnp.where(last, 0.0, glu(unext_ref))
    n_taps = dww_ref.shape[0]
    base = CONV_HALO - n_taps // 2
    sh_rows = zsh_ref.shape[1]
    for r in range(1, SUBLANES):
        zsh_ref[r - 1] = zext_ref[pl.ds(r, sh_rows), :]
    z = dwb_ref[...]
    for k in range(n_taps):
        q, r = divmod(base + k, SUBLANES)
        rows = pl.ds(q * SUBLANES, tm)
        tap = zext_ref[rows, :] if r == 0 else zsh_ref[r - 1, rows, :]
        z = z + dww_ref[k:k + 1, :] * tap
    mu = jnp.mean(z, axis=-1, keepdims=True)
    zc = z - mu
    var = jnp.mean(zc * zc, axis=-1, keepdims=True)
    zn = zc * lax.rsqrt(var + LN_EPS) * lng_ref[...] + lnb_ref[...]
    y_ref[:, pool_w:pool_w + conv_w] = (zn * jax.nn.sigmoid(zn)).astype(BF16)

    xext_ref[0:SUBLANES, :] = jnp.where(first, 0.0, xrprev_ref[...])
    xext_ref[SUBLANES:, :] = xr_ref[...]
    lru_taps = cw_ref.shape[0]
    offsets = tuple(SUBLANES - (lru_taps - 1) + k for k in range(lru_taps))
    _lru_coeffs(xext_ref, offsets, cw_ref, cb_ref, wax_ref, bax_ref, lam_ref, a_ref, b_ref)
    _lru_scan(a_ref, b_ref, hf_ref, carry_ref, reverse=False)
    gate = jax.nn.gelu(ua_ref[:, lg0:lg0 + lru_w])
    y_ref[:, pool_w + conv_w:] = ((hf_ref[...] + hb_ref[...]) * gate).astype(BF16)


def _mix_body(*refs, seq_len, n_tiles):
    (x_ref, ua_ref, uprev_ref, unext_ref, xr_ref, xrprev_ref, hb_ref,
     pw_ref, ps_ref, dww_ref, dwb_ref, lng_ref, lnb_ref,
     cw_ref, cb_ref, wax_ref, bax_ref, lam_ref, wout_ref, o_ref,
     pext_ref, zext_ref, zsh_ref, xext_ref, a_ref, b_ref, hf_ref, y0_ref, y1_ref, carry_ref) = refs
    step = pl.program_id(0)

    @pl.when(step == 0)
    def _():
        carry_ref[...] = jnp.zeros_like(carry_ref)
        y1_ref[...] = jnp.zeros_like(y1_ref)

    run = functools.partial(
        _mix_step, x_ref=x_ref, ua_ref=ua_ref, uprev_ref=uprev_ref, unext_ref=unext_ref, xr_ref=xr_ref,
        xrprev_ref=xrprev_ref, hb_ref=hb_ref, pw_ref=pw_ref, ps_ref=ps_ref, dww_ref=dww_ref,
        dwb_ref=dwb_ref, lng_ref=lng_ref, lnb_ref=lnb_ref, cw_ref=cw_ref, cb_ref=cb_ref, wax_ref=wax_ref,
        bax_ref=bax_ref, lam_ref=lam_ref, wout_ref=wout_ref, o_ref=o_ref, pext_ref=pext_ref,
        zext_ref=zext_ref, zsh_ref=zsh_ref, xext_ref=xext_ref, a_ref=a_ref, b_ref=b_ref, hf_ref=hf_ref,
        carry_ref=carry_ref, seq_len=seq_len, n_tiles=n_tiles)

    @pl.when(step % 2 == 0)
    def _():
        run(y1_ref, y0_ref)

    @pl.when(step % 2 == 1)
    def _():
        run(y0_ref, y1_ref)


def _mix(x, ua, xr, hb, pool_w, pool_scale, dw_w, dw_b, ln_g, ln_b, cw, cb, wax, bax, lam, w_out_bf16,
         layer, *, tm):
    s, d = x.shape
    na = ua.shape[1]
    c = xr.shape[1]
    tm = min(tm, s)
    nt = s // tm
    n_pool_groups, pg = pool_w.shape[1], pool_w.shape[2]
    pool_width = n_pool_groups * pg
    conv_width = dw_w.shape[2]
    conv_taps = dw_w.shape[1]
    lru_taps = cw.shape[2]
    n_heads, hd = wax.shape[2], wax.shape[3]
    d_mix = w_out_bf16.shape[1]
    assert conv_taps // 2 < CONV_HALO and max(POOL_WINDOWS) // 2 <= POOL_HALO and lru_taps - 1 <= SUBLANES
    assert tm % CONV_HALO == 0 and pool_width + conv_width + c == d_mix
    ch_per_tile = tm // CONV_HALO
    sl_per_tile = tm // SUBLANES
    last_ch = s // CONV_HALO - 1
    sh_rows = tm + ((CONV_HALO + conv_taps // 2) // SUBLANES) * SUBLANES

    cur = lambda st: (jnp.minimum(st, nt - 1), 0)
    prev = lambda st: (jnp.maximum(st - 1, 0), 0)
    body = functools.partial(_mix_body, seq_len=s, n_tiles=nt)
    return pl.pallas_call(
        body,
        grid=(nt + 1,),
        in_specs=[
            pl.BlockSpec((tm, d), prev),
            pl.BlockSpec((tm, na), cur),
            pl.BlockSpec((CONV_HALO, na),
                         lambda st: (jnp.maximum(jnp.minimum(st, nt - 1) * ch_per_tile - 1, 0), 0)),
            pl.BlockSpec((CONV_HALO, na),
                         lambda st: (jnp.minimum((jnp.minimum(st, nt - 1) + 1) * ch_per_tile, last_ch), 0)),
            pl.BlockSpec((tm, c), cur),
            pl.BlockSpec((SUBLANES, c),
                         lambda st: (jnp.maximum(jnp.minimum(st, nt - 1) * sl_per_tile - 1, 0), 0)),
            pl.BlockSpec((tm, c), cur),
            _layer_block((n_pool_groups, pg, pg), layer),
            _layer_block((1, pool_width), layer),
            _layer_block((conv_taps, conv_width), layer),
            _layer_block((1, conv_width), layer),
            _layer_block((1, conv_width), layer),
            _layer_block((1, conv_width), layer),
            _dir_block((lru_taps, c), layer, 0),
            _dir_block((1, c), layer, 0),
            _dir_block((n_heads, hd, 2 * hd), layer, 0),
            _dir_block((n_heads, 1, 2 * hd), layer, 0),
            _dir_block((1, c), layer, 0),
            _layer_block((d_mix, d), layer),
        ],
        out_specs=pl.BlockSpec((tm, d), prev),
        out_shape=jax.ShapeDtypeStruct((s, d), F32),
        scratch_shapes=[
            pltpu.VMEM((tm + 2 * POOL_HALO, pool_width), F32),
            pltpu.VMEM((tm + 2 * CONV_HALO, conv_width), F32),
            pltpu.VMEM((SUBLANES - 1, sh_rows, conv_width), F32),
            pltpu.VMEM((tm + SUBLANES, c), F32),
            pltpu.VMEM((tm, c), F32),
            pltpu.VMEM((tm, c), F32),
            pltpu.VMEM((tm, c), F32),
            pltpu.VMEM((tm, d_mix), BF16),
            pltpu.VMEM((tm, d_mix), BF16),
            pltpu.VMEM((SUBLANES, c), F32),
        ],
        compiler_params=pltpu.CompilerParams(
            dimension_semantics=("arbitrary",), vmem_limit_bytes=V7X_VMEM_LIMIT_BYTES),
        name="mix",
    )(x, ua, ua, ua, xr, xr, hb, pool_w, pool_scale, dw_w, dw_b, ln_g, ln_b,
      cw, cb, wax, bax, lam, w_out_bf16)


def kernel(x, norm_ffn1, ffn1_w_gate, ffn1_w_up, ffn1_w_down, norm_mix, w_in, pool_w, pool_scale, conv_dw_w, conv_dw_b, conv_ln_g, conv_ln_b, lru_conv_w, lru_conv_b, lru_w_a, lru_b_a, lru_w_x, lru_b_x, lru_lambda, w_out, norm_ffn2, ffn2_w_gate, ffn2_w_up, ffn2_w_down, norm_final):
    b, s, d = x.shape
    depth = w_in.shape[0]
    n_dirs, n_heads, hd = lru_w_a.shape[1], lru_w_a.shape[2], lru_w_a.shape[3]

    row = lambda p: p[:, None, :]
    norm_ffn1_r, norm_mix_r, norm_ffn2_r = row(norm_ffn1), row(norm_mix), row(norm_ffn2)
    norm_final_r = norm_final[None, :]
    pool_scale_r, dw_b_r, ln_g_r, ln_b_r = row(pool_scale), row(conv_dw_b), row(conv_ln_g), row(conv_ln_b)
    lru_cb_r = lru_conv_b[:, :, None, :]
    lam_r = lru_lambda[:, :, None, :]
    wax = jnp.concatenate([lru_w_a, lru_w_x], axis=-1).astype(BF16)
    bax = jnp.concatenate([lru_b_a.reshape(depth, n_dirs, n_heads, 1, hd),
                           lru_b_x.reshape(depth, n_dirs, n_heads, 1, hd)], axis=-1)
    w_in_b = w_in.astype(BF16)
    w_out_b = w_out.astype(BF16)
    pool_w_b = pool_w.astype(BF16)

    outs = []
    for bi in range(b):
        xb = x[bi]
        for l in range(depth):
            xb = _ffn(xb, norm_ffn1_r, ffn1_w_gate, ffn1_w_up, ffn1_w_down, norm_final_r, l,
                      final_norm=False, tm=FFN_ROWS, tf=FFN_COLS)
            ua, xr, hb = _proj_bwd(xb, norm_mix_r, w_in_b, lru_conv_w, lru_cb_r, wax, bax, lam_r, l,
                                   tm=MIX_ROWS)
            xb = _mix(xb, ua, xr, hb, pool_w_b, pool_scale_r, conv_dw_w, dw_b_r, ln_g_r, ln_b_r,
                      lru_conv_w, lru_cb_r, wax, bax, lam_r, w_out_b, l, tm=MIX_ROWS)
            xb = _ffn(xb, norm_ffn2_r, ffn2_w_gate, ffn2_w_up, ffn2_w_down, norm_final_r, l,
                      final_norm=(l == depth - 1), tm=FFN_ROWS, tf=FFN_COLS)
        outs.append(xb)
    return jnp.stack(outs, axis=0) if b > 1 else outs[0][None]
```

```python
import functools

import jax
import jax.numpy as jnp
from jax import lax
from jax.experimental import pallas as pl
from jax.experimental.pallas import tpu as pltpu

RMS_EPS = 1e-6
LN_EPS = 1e-5
LRU_C = 8.0
POOL_WINDOWS = (2, 4, 8, 16)

SUBLANES = 8
HALO_GROUPS = 16
LRU_HALO_GROUPS = 4
V7X_VMEM_LIMIT_BYTES = 56 * 1024 * 1024

F32 = jnp.float32
BF16 = jnp.bfloat16

FFN_ROWS = 1024
FFN_COLS = 512
MIX_ROWS = 256


def _rms(x, g):
    return x * lax.rsqrt(jnp.mean(x * x, axis=-1, keepdims=True) + RMS_EPS) * g


def _layer_block(shape, layer):
    zeros = (0,) * len(shape)
    return pl.BlockSpec((None,) + tuple(shape), lambda *_: (layer,) + zeros, pipeline_mode=pl.Buffered(1))


def _dir_block(shape, layer, direction):
    zeros = (0,) * len(shape)
    return pl.BlockSpec((None, None) + tuple(shape), lambda *_: (layer, direction) + zeros,
                        pipeline_mode=pl.Buffered(1))


def _ffn_body(x_hbm, g_ref, wg_ref, wu_ref, wd_ref, gf_ref, o_hbm, acc_ref, h_ref, in_sem, out_sem,
              *, n_ff, d_chunk, final_norm):
    i = pl.program_id(0)
    j = pl.program_id(1)
    n_tiles = pl.num_programs(0)
    tm = acc_ref.shape[1]
    slot = i % 2
    other = 1 - slot

    def x_copy(tile, s):
        return pltpu.make_async_copy(x_hbm.at[pl.ds(tile * tm, tm)], acc_ref.at[s], in_sem.at[s])

    def o_copy(tile, s):
        return pltpu.make_async_copy(acc_ref.at[s], o_hbm.at[pl.ds(tile * tm, tm)], out_sem.at[s])

    @pl.when(j == 0)
    def _():
        @pl.when(i == 0)
        def _():
            x_copy(0, 0).start()

        x_copy(i, slot).wait()
        h_ref[...] = _rms(acc_ref[slot], g_ref[...]).astype(BF16)

    @pl.when(j == 1)
    def _():
        @pl.when(i >= 1)
        def _():
            o_copy(i - 1, other).wait()

        @pl.when(i + 1 < n_tiles)
        def _():
            x_copy(i + 1, other).start()

    h = h_ref[...]
    gate = jnp.dot(h, wg_ref[...].astype(BF16), preferred_element_type=F32)
    up = jnp.dot(h, wu_ref[...].astype(BF16), preferred_element_type=F32)
    act = ((gate * jax.nn.sigmoid(gate)) * (0.5 * up)).astype(BF16)
    d = acc_ref.shape[2]
    for c in range(0, d, d_chunk):
        acc_ref[slot, :, c:c + d_chunk] += jnp.dot(act, wd_ref[:, c:c + d_chunk].astype(BF16),
                                                   preferred_element_type=F32)

    @pl.when(j == n_ff - 1)
    def _():
        if final_norm:
            acc_ref[slot] = _rms(acc_ref[slot], gf_ref[...])
        o_copy(i, slot).start()

        @pl.when(i == n_tiles - 1)
        def _():
            o_copy(i, slot).wait()


def _ffn(x, norm_g, w_gate, w_up, w_down, norm_final, layer, *, final_norm, tm, tf):
    s, d = x.shape
    f = w_gate.shape[-1]
    tm = min(tm, s)
    tf = min(tf, f)
    n_ff = f // tf
    assert n_ff >= 2
    body = functools.partial(_ffn_body, n_ff=n_ff, d_chunk=min(512, d), final_norm=final_norm)
    return pl.pallas_call(
        body,
        grid=(s // tm, n_ff),
        in_specs=[
            pl.BlockSpec(memory_space=pl.ANY),
            pl.BlockSpec((None, 1, d), lambda i, j: (layer, 0, 0)),
            pl.BlockSpec((None, d, tf), lambda i, j: (layer, 0, j)),
            pl.BlockSpec((None, d, tf), lambda i, j: (layer, 0, j)),
            pl.BlockSpec((None, tf, d), lambda i, j: (layer, j, 0)),
            pl.BlockSpec((1, d), lambda i, j: (0, 0)),
        ],
        out_specs=pl.BlockSpec(memory_space=pl.ANY),
        out_shape=jax.ShapeDtypeStruct((s, d), F32),
        scratch_shapes=[pltpu.VMEM((2, tm, d), F32), pltpu.VMEM((tm, d), BF16),
                        pltpu.SemaphoreType.DMA((2,)), pltpu.SemaphoreType.DMA((2,))],
        compiler_params=pltpu.CompilerParams(
            dimension_semantics=("arbitrary", "arbitrary"),
            vmem_limit_bytes=V7X_VMEM_LIMIT_BYTES),
        name="ffn",
    )(x, norm_g, w_gate, w_up, w_down, norm_final)


def _lru_coeffs(ext_ref, offsets, cw_ref, cb_ref, wax_ref, bax_ref, lam_ref, a_ref, b_ref):
    tm, c = a_ref.shape
    n_heads = wax_ref.shape[0]
    hd = c // n_heads
    xc = cb_ref[...] + cw_ref[0:1, :] * ext_ref[pl.ds(offsets[0], tm), :]
    for k in range(1, len(offsets)):
        xc = xc + cw_ref[k:k + 1, :] * ext_ref[pl.ds(offsets[k], tm), :]
    softplus_neg_lam = jax.nn.softplus(-lam_ref[...])
    for h in range(n_heads):
        sl = slice(h * hd, (h + 1) * hd)
        xh = xc[:, sl]
        ri = jnp.dot(xh.astype(BF16), wax_ref[h], preferred_element_type=F32) + bax_ref[h]
        r = jax.nn.sigmoid(ri[:, :hd])
        ig = jax.nn.sigmoid(ri[:, hd:])
        log_a = -LRU_C * r * softplus_neg_lam[:, sl]
        a = jnp.exp(log_a)
        a_ref[:, sl] = a
        b_ref[:, sl] = jnp.sqrt(1.0 - a * a) * (ig * xh)


def _sublane_scan(a, b, *, reverse):
    row = lax.broadcasted_iota(jnp.int32, a.shape, 0)
    for dist in (1, 2, 4):
        if reverse:
            keep = row < SUBLANES - dist
            shift = SUBLANES - dist
        else:
            keep = row >= dist
            shift = dist
        a_n = jnp.where(keep, pltpu.roll(a, shift, 0), 1.0)
        b_n = jnp.where(keep, pltpu.roll(b, shift, 0), 0.0)
        b = a * b_n + b
        a = a * a_n
    return a, b


def _lru_scan(a_ref, b_ref, h_ref, carry_ref, *, reverse):
    tm, c = a_ref.shape
    n_groups = tm // SUBLANES
    order = range(n_groups - 1, -1, -1) if reverse else range(n_groups)
    rows = lambda j: slice(j * SUBLANES, (j + 1) * SUBLANES)
    row = lax.broadcasted_iota(jnp.int32, (SUBLANES, c), 0)

    h = jnp.zeros((SUBLANES, c), F32)
    p = jnp.ones((SUBLANES, c), F32)
    for j in order:
        a = a_ref[rows(j), :]
        h = a * h + b_ref[rows(j), :]
        p = a * p
    a_cum, b_cum = _sublane_scan(p, h, reverse=reverse)
    carry = carry_ref[...]
    h_edge = a_cum * carry + b_cum
    if reverse:
        h = jnp.where(row == SUBLANES - 1, carry, pltpu.roll(h_edge, SUBLANES - 1, 0))
        carry_ref[...] = jnp.broadcast_to(h_edge[0:1, :], (SUBLANES, c))
    else:
        h = jnp.where(row == 0, carry, pltpu.roll(h_edge, 1, 0))
        carry_ref[...] = jnp.broadcast_to(h_edge[SUBLANES - 1:SUBLANES, :], (SUBLANES, c))
    for j in order:
        h = a_ref[rows(j), :] * h + b_ref[rows(j), :]
        h_ref[rows(j), :] = h


def _wrap_rows(dst_ref, dst_row, inner, outer, *, forward):
    row = lax.broadcasted_iota(jnp.int32, inner.shape, 0)
    if forward:
        val = jnp.where(row == SUBLANES - 1, pltpu.roll(outer, SUBLANES - 1, 0),
                        pltpu.roll(inner, SUBLANES - 1, 0))
    else:
        val = jnp.where(row == 0, pltpu.roll(outer, 1, 0), pltpu.roll(inner, 1, 0))
    dst_ref[dst_row:dst_row + SUBLANES, :] = val


def _proj_bwd_body(x_ref, g_ref, perm_ref, w_ref, cw_ref, cb_ref, wax_ref, bax_ref, lam_ref,
                   ua_ref, xr_ref, hb_ref, ext_ref, a_ref, b_ref, carry_ref, *, pool_w, conv_w):
    i = pl.program_id(0)
    tm = x_ref.shape[0]
    c = xr_ref.shape[1]
    n_in = w_ref.shape[1] - c
    n_taps = cw_ref.shape[0]
    halo = (n_taps - 1) * SUBLANES

    @pl.when(i == 0)
    def _():
        carry_ref[...] = jnp.zeros_like(carry_ref)
        ext_ref[0:halo, :] = jnp.zeros((halo, c), F32)

    h = _rms(x_ref[...], g_ref[...]).astype(BF16)
    hp = jnp.dot(perm_ref[...], h, preferred_element_type=F32).astype(BF16)
    xr = jnp.dot(hp, w_ref[:, n_in:], preferred_element_type=F32)
    u = jnp.dot(hp, w_ref[:, :n_in], preferred_element_type=F32)
    v0, g0, lg0 = pool_w, pool_w + conv_w, pool_w + 2 * conv_w
    ua_ref[:, 0:pool_w] = u[:, 0:pool_w]
    ua_ref[:, pool_w:pool_w + conv_w] = u[:, v0:v0 + conv_w] * jax.nn.sigmoid(u[:, g0:g0 + conv_w])
    ua_ref[:, pool_w + conv_w:] = jax.nn.gelu(u[:, lg0:])
    xr_ref[...] = xr
    for j in range(n_taps - 1):
        rows = slice(j * SUBLANES, (j + 1) * SUBLANES)
        _wrap_rows(ext_ref, tm + j * SUBLANES, xr[rows, :], ext_ref[rows, :], forward=True)
    ext_ref[0:tm, :] = xr
    offsets = tuple((n_taps - 1 - k) * SUBLANES for k in range(n_taps))
    _lru_coeffs(ext_ref, offsets, cw_ref, cb_ref, wax_ref, bax_ref, lam_ref, a_ref, b_ref)
    _lru_scan(a_ref, b_ref, hb_ref, carry_ref, reverse=True)


def _proj_bwd(x, norm_g, perm, w_in_bf16, cw, cb, wax, bax, lam, layer, *, tm, pool_w, conv_w):
    s, d = x.shape
    d_in = w_in_bf16.shape[-1]
    n_taps = cw.shape[2]
    n_heads, hd = wax.shape[2], wax.shape[3]
    c = n_heads * hd
    na = d_in - c - conv_w
    nt = s // tm
    rev = lambda i: (nt - 1 - i, 0)
    body = functools.partial(_proj_bwd_body, pool_w=pool_w, conv_w=conv_w)
    return pl.pallas_call(
        body,
        grid=(nt,),
        in_specs=[
            pl.BlockSpec((tm, d), rev),
            _layer_block((1, d), layer),
            pl.BlockSpec((tm, tm), lambda i: (0, 0), pipeline_mode=pl.Buffered(1)),
            _layer_block((d, d_in), layer),
            _dir_block((n_taps, c), layer, 1),
            _dir_block((1, c), layer, 1),
            _dir_block((n_heads, hd, 2 * hd), layer, 1),
            _dir_block((n_heads, 1, 2 * hd), layer, 1),
            _dir_block((1, c), layer, 1),
        ],
        out_specs=[pl.BlockSpec((tm, na), rev), pl.BlockSpec((tm, c), rev), pl.BlockSpec((tm, c), rev)],
        out_shape=[jax.ShapeDtypeStruct((s, na), F32), jax.ShapeDtypeStruct((s, c), F32),
                   jax.ShapeDtypeStruct((s, c), F32)],
        scratch_shapes=[pltpu.VMEM((tm + (n_taps - 1) * SUBLANES, c), F32), pltpu.VMEM((tm, c), F32),
                        pltpu.VMEM((tm, c), F32), pltpu.VMEM((SUBLANES, c), F32)],
        compiler_params=pltpu.CompilerParams(
            dimension_semantics=("arbitrary",), vmem_limit_bytes=V7X_VMEM_LIMIT_BYTES),
        name="proj_bwd",
    )(x, norm_g, perm, w_in_bf16, cw, cb, wax, bax, lam)


def _mix_step(y_prev_ref, y_ref, x_ref, ua_ref, uprev_ref, unext_ref, xr_ref, xrprev_ref, hb_ref,
              pw_ref, ps_ref, dww_ref, dwb_ref, lng_ref, lnb_ref,
              cw_ref, cb_ref, wax_ref, bax_ref, lam_ref, wout_ref, unperm_ref, o_ref,
              pzext_ref, xext_ref, a_ref, b_ref, hf_ref, yp_ref, carry_ref, *, seq_len, n_tiles):
    step = pl.program_id(0)
    i = jnp.minimum(step, n_tiles - 1)
    tm = ua_ref.shape[0]
    chunk = tm // SUBLANES
    first = i == 0
    last = i == n_tiles - 1
    n_groups, pg = pw_ref.shape[0], pw_ref.shape[1]
    pool_w = n_groups * pg
    conv_w = dww_ref.shape[1]
    pz_w = pool_w + conv_w
    hv = HALO_GROUPS
    rows = lambda j: slice(j * SUBLANES, (j + 1) * SUBLANES)

    o_ref[...] = x_ref[...] + jnp.dot(y_prev_ref[...], wout_ref[...], preferred_element_type=F32)

    for j in range(hv):
        before = jnp.where(first, 0.0, uprev_ref[rows(j), :])
        _wrap_rows(pzext_ref, j * SUBLANES, ua_ref[rows(chunk - hv + j), 0:pz_w], before, forward=False)
        after = jnp.where(last, 0.0, unext_ref[rows(j), :])
        _wrap_rows(pzext_ref, (hv + chunk + j) * SUBLANES, ua_ref[rows(j), 0:pz_w], after, forward=True)
    pzext_ref[hv * SUBLANES:hv * SUBLANES + tm, :] = ua_ref[:, 0:pz_w]

    pos = lax.broadcasted_iota(jnp.int32, (tm, pg), 0)
    t = i * tm + (pos & (SUBLANES - 1)) * chunk + (pos >> 3)
    for g, win in enumerate(POOL_WINDOWS):
        sl = slice(g * pg, (g + 1) * pg)
        half = win // 2
        ssum = pzext_ref[pl.ds((hv - half) * SUBLANES, tm), sl]
        for o in range(-half + 1, half):
            ssum = ssum + pzext_ref[pl.ds((hv + o) * SUBLANES, tm), sl]
        cnt = (jnp.minimum(t + half, seq_len) - jnp.maximum(t - half, 0)).astype(F32)
        pooled = ssum / cnt - ua_ref[:, sl]
        mixed = jnp.dot(pooled.astype(BF16), pw_ref[g], preferred_element_type=F32)
        yp_ref[:, sl] = (mixed * ps_ref[:, sl]).astype(BF16)

    n_taps = dww_ref.shape[0]
    pad = n_taps // 2
    z = dwb_ref[...]
    for k in range(n_taps):
        z = z + dww_ref[k:k + 1, :] * pzext_ref[pl.ds((hv - pad + k) * SUBLANES, tm), pool_w:pz_w]
    mu = jnp.mean(z, axis=-1, keepdims=True)
    zc = z - mu
    var = jnp.mean(zc * zc, axis=-1, keepdims=True)
    zn = zc * lax.rsqrt(var + LN_EPS) * lng_ref[...] + lnb_ref[...]
    yp_ref[:, pool_w:pz_w] = (zn * jax.nn.sigmoid(zn)).astype(BF16)

    lru_taps = cw_ref.shape[0]
    lh = lru_taps - 1
    n_prev = xrprev_ref.shape[0] // SUBLANES
    for j in range(lh):
        before = jnp.where(first, 0.0, xrprev_ref[rows(n_prev - lh + j), :])
        _wrap_rows(xext_ref, j * SUBLANES, xr_ref[rows(chunk - lh + j), :], before, forward=False)
    xext_ref[lh * SUBLANES:, :] = xr_ref[...]
    offsets = tuple(k * SUBLANES for k in range(lru_taps))
    _lru_coeffs(xext_ref, offsets, cw_ref, cb_ref, wax_ref, bax_ref, lam_ref, a_ref, b_ref)
    _lru_scan(a_ref, b_ref, hf_ref, carry_ref, reverse=False)
    yp_ref[:, pz_w:] = ((hf_ref[...] + hb_ref[...]) * ua_ref[:, pz_w:]).astype(BF16)

    y_ref[...] = jnp.dot(unperm_ref[...], yp_ref[...], preferred_element_type=F32).astype(BF16)


def _mix_body(*refs, seq_len, n_tiles):
    (x_ref, ua_ref, uprev_ref, unext_ref, xr_ref, xrprev_ref, hb_ref,
     pw_ref, ps_ref, dww_ref, dwb_ref, lng_ref, lnb_ref,
     cw_ref, cb_ref, wax_ref, bax_ref, lam_ref, wout_ref, unperm_ref, o_ref,
     pzext_ref, xext_ref, a_ref, b_ref, hf_ref, yp_ref, y0_ref, y1_ref, carry_ref) = refs
    step = pl.program_id(0)

    @pl.when(step == 0)
    def _():
        carry_ref[...] = jnp.zeros_like(carry_ref)
        y1_ref[...] = jnp.zeros_like(y1_ref)

    run = functools.partial(
        _mix_step, x_ref=x_ref, ua_ref=ua_ref, uprev_ref=uprev_ref, unext_ref=unext_ref, xr_ref=xr_ref,
        xrprev_ref=xrprev_ref, hb_ref=hb_ref, pw_ref=pw_ref, ps_ref=ps_ref, dww_ref=dww_ref,
        dwb_ref=dwb_ref, lng_ref=lng_ref, lnb_ref=lnb_ref, cw_ref=cw_ref, cb_ref=cb_ref, wax_ref=wax_ref,
        bax_ref=bax_ref, lam_ref=lam_ref, wout_ref=wout_ref, unperm_ref=unperm_ref, o_ref=o_ref,
        pzext_ref=pzext_ref, xext_ref=xext_ref, a_ref=a_ref, b_ref=b_ref, hf_ref=hf_ref, yp_ref=yp_ref,
        carry_ref=carry_ref, seq_len=seq_len, n_tiles=n_tiles)

    @pl.when(step % 2 == 0)
    def _():
        run(y1_ref, y0_ref)

    @pl.when(step % 2 == 1)
    def _():
        run(y0_ref, y1_ref)


def _mix(x, ua, xr, hb, unperm, pool_w, pool_scale, dw_w, dw_b, ln_g, ln_b, cw, cb, wax, bax, lam,
         w_out_bf16, layer, *, tm):
    s, d = x.shape
    na = ua.shape[1]
    c = xr.shape[1]
    nt = s // tm
    n_pool_groups, pg = pool_w.shape[1], pool_w.shape[2]
    pool_width = n_pool_groups * pg
    conv_width = dw_w.shape[2]
    conv_taps = dw_w.shape[1]
    lru_taps = cw.shape[2]
    n_heads, hd = wax.shape[2], wax.shape[3]
    d_mix = w_out_bf16.shape[1]
    pz_w = pool_width + conv_width
    halo_rows = HALO_GROUPS * SUBLANES
    lru_halo_rows = LRU_HALO_GROUPS * SUBLANES
    assert conv_taps // 2 <= HALO_GROUPS and max(POOL_WINDOWS) // 2 <= HALO_GROUPS
    assert lru_taps - 1 <= LRU_HALO_GROUPS and pz_w + c == d_mix == na
    assert tm % halo_rows == 0 and pz_w % 128 == 0
    halo_per_tile = tm // halo_rows
    lru_per_tile = tm // lru_halo_rows
    last_halo = s // halo_rows - 1

    tile = lambda st: jnp.minimum(st, nt - 1)
    cur = lambda st: (tile(st), 0)
    prev = lambda st: (jnp.maximum(st - 1, 0), 0)
    body = functools.partial(_mix_body, seq_len=s, n_tiles=nt)
    return pl.pallas_call(
        body,
        grid=(nt + 1,),
        in_specs=[
            pl.BlockSpec((tm, d), prev),
            pl.BlockSpec((tm, na), cur),
            pl.BlockSpec((halo_rows, pz_w), lambda st: (jnp.maximum(tile(st) * halo_per_tile - 1, 0), 0)),
            pl.BlockSpec((halo_rows, pz_w),
                         lambda st: (jnp.minimum((tile(st) + 1) * halo_per_tile, last_halo), 0)),
            pl.BlockSpec((tm, c), cur),
            pl.BlockSpec((lru_halo_rows, c), lambda st: (jnp.maximum(tile(st) * lru_per_tile - 1, 0), 0)),
            pl.BlockSpec((tm, c), cur),
            _layer_block((n_pool_groups, pg, pg), layer),
            _layer_block((1, pool_width), layer),
            _layer_block((conv_taps, conv_width), layer),
            _layer_block((1, conv_width), layer),
            _layer_block((1, conv_width), layer),
            _layer_block((1, conv_width), layer),
            _dir_block((lru_taps, c), layer, 0),
            _dir_block((1, c), layer, 0),
            _dir_block((n_heads, hd, 2 * hd), layer, 0),
            _dir_block((n_heads, 1, 2 * hd), layer, 0),
            _dir_block((1, c), layer, 0),
            _layer_block((d_mix, d), layer),
            pl.BlockSpec((tm, tm), lambda st: (0, 0), pipeline_mode=pl.Buffered(1)),
        ],
        out_specs=pl.BlockSpec((tm, d), prev),
        out_shape=jax.ShapeDtypeStruct((s, d), F32),
        scratch_shapes=[
            pltpu.VMEM((tm + 2 * halo_rows, pz_w), F32),
            pltpu.VMEM((tm + (lru_taps - 1) * SUBLANES, c), F32),
            pltpu.VMEM((tm, c), F32),
            pltpu.VMEM((tm, c), F32),
            pltpu.VMEM((tm, c), F32),
            pltpu.VMEM((tm, d_mix), BF16),
            pltpu.VMEM((tm, d_mix), BF16),
            pltpu.VMEM((tm, d_mix), BF16),
            pltpu.VMEM((SUBLANES, c), F32),
        ],
        compiler_params=pltpu.CompilerParams(
            dimension_semantics=("arbitrary",), vmem_limit_bytes=V7X_VMEM_LIMIT_BYTES),
        name="mix",
    )(x, ua, ua, ua, xr, xr, hb, pool_w, pool_scale, dw_w, dw_b, ln_g, ln_b,
      cw, cb, wax, bax, lam, w_out_bf16, unperm)


def _chunk_permutation(tm):
    chunk = tm // SUBLANES
    pos = jnp.arange(tm)
    src = (pos % SUBLANES) * chunk + pos // SUBLANES
    return (src[:, None] == jnp.arange(tm)[None, :]).astype(BF16)


def kernel(x, norm_ffn1, ffn1_w_gate, ffn1_w_up, ffn1_w_down, norm_mix, w_in, pool_w, pool_scale, conv_dw_w, conv_dw_b, conv_ln_g, conv_ln_b, lru_conv_w, lru_conv_b, lru_w_a, lru_b_a, lru_w_x, lru_b_x, lru_lambda, w_out, norm_ffn2, ffn2_w_gate, ffn2_w_up, ffn2_w_down, norm_final):
    b, s, d = x.shape
    depth = w_in.shape[0]
    n_dirs, n_heads, hd = lru_w_a.shape[1], lru_w_a.shape[2], lru_w_a.shape[3]

    row = lambda p: p[:, None, :]
    norm_ffn1_r, norm_mix_r, norm_ffn2_r = row(norm_ffn1), row(norm_mix), row(norm_ffn2)
    norm_final_r = norm_final[None, :]
    pool_scale_r, dw_b_r, ln_g_r, ln_b_r = row(pool_scale), row(conv_dw_b), row(conv_ln_g), row(conv_ln_b)
    lru_cb_r = lru_conv_b[:, :, None, :]
    lam_r = lru_lambda[:, :, None, :]
    wax = jnp.concatenate([lru_w_a, lru_w_x], axis=-1).astype(BF16)
    bax = jnp.concatenate([lru_b_a.reshape(depth, n_dirs, n_heads, 1, hd),
                           lru_b_x.reshape(depth, n_dirs, n_heads, 1, hd)], axis=-1)
    w_in_b = w_in.astype(BF16)
    w_out_b = w_out.astype(BF16)
    pool_w_b = pool_w.astype(BF16)
    tm_mix = min(MIX_ROWS, s)
    perm = _chunk_permutation(tm_mix)
    unperm = perm.T
    pool_width = pool_w.shape[1] * pool_w.shape[2]
    conv_width = conv_dw_w.shape[2]

    outs = []
    for bi in range(b):
        xb = x[bi]
        for l in range(depth):
            xb = _ffn(xb, norm_ffn1_r, ffn1_w_gate, ffn1_w_up, ffn1_w_down, norm_final_r, l,
                      final_norm=False, tm=FFN_ROWS, tf=FFN_COLS)
            ua, xr, hb = _proj_bwd(xb, norm_mix_r, perm, w_in_b, lru_conv_w, lru_cb_r, wax, bax, lam_r, l,
                                   tm=tm_mix, pool_w=pool_width, conv_w=conv_width)
            xb = _mix(xb, ua, xr, hb, unperm, pool_w_b, pool_scale_r, conv_dw_w, dw_b_r, ln_g_r, ln_b_r,
                      lru_conv_w, lru_cb_r, wax, bax, lam_r, w_out_b, l, tm=tm_mix)
            xb = _ffn(xb, norm_ffn2_r, ffn2_w_gate, ffn2_w_up, ffn2_w_down, norm_final_r, l,
                      final_norm=(l == depth - 1), tm=FFN_ROWS, tf=FFN_COLS)
        outs.append(xb)
    return jnp.stack(outs, axis=0) if b > 1 else outs[0][None]
```

```python
import functools

import jax
import jax.numpy as jnp
from jax import lax
from jax.experimental import pallas as pl
from jax.experimental.pallas import tpu as pltpu

RMS_EPS = 1e-6
LN_EPS = 1e-5
LRU_C = 8.0
POOL_WINDOWS = (2, 4, 8, 16)

SUBLANES = 8
HALO_GROUPS = 16
LRU_HALO_GROUPS = 4
V7X_VMEM_LIMIT_BYTES = 56 * 1024 * 1024

F32 = jnp.float32
BF16 = jnp.bfloat16

FFN_ROWS = 1024
FFN_COLS = 512
MIX_ROWS = 256


def _rms(x, g):
    return x * lax.rsqrt(jnp.mean(x * x, axis=-1, keepdims=True) + RMS_EPS) * g


def _layer_block(shape, layer):
    zeros = (0,) * len(shape)
    return pl.BlockSpec((None,) + tuple(shape), lambda *_: (layer,) + zeros, pipeline_mode=pl.Buffered(1))


def _dir_block(shape, layer, direction):
    zeros = (0,) * len(shape)
    return pl.BlockSpec((None, None) + tuple(shape), lambda *_: (layer, direction) + zeros,
                        pipeline_mode=pl.Buffered(1))


def _ffn_step(acc_ref, acc_next_ref, h_ref, h_next_ref, in_sem, out_sem, in_next_sem, out_prev_sem,
              x_hbm, g_ref, wg_ref, wu_ref, wd_ref, gf_ref, o_hbm, *, n_ff, n_norm_chunks, d_chunk, final_norm):
    i = pl.program_id(0)
    j = pl.program_id(1)
    n_tiles = pl.num_programs(0)
    tm, d = acc_ref.shape

    def x_next_copy():
        return pltpu.make_async_copy(x_hbm.at[pl.ds((i + 1) * tm, tm)], acc_next_ref, in_next_sem)

    def o_prev_copy():
        return pltpu.make_async_copy(acc_next_ref, o_hbm.at[pl.ds((i - 1) * tm, tm)], out_prev_sem)

    def o_copy():
        return pltpu.make_async_copy(acc_ref, o_hbm.at[pl.ds(i * tm, tm)], out_sem)

    @pl.when((i == 0) & (j == 0))
    def _():
        first = pltpu.make_async_copy(x_hbm.at[pl.ds(0, tm)], acc_ref, in_sem)
        first.start()
        first.wait()
        h_ref[...] = _rms(acc_ref[...], g_ref[...]).astype(BF16)

    @pl.when((j == 1) & (i >= 1))
    def _():
        o_prev_copy().wait()

    @pl.when((j == 2) & (i + 1 < n_tiles))
    def _():
        x_next_copy().wait()

    h = h_ref[...]
    gate = jnp.dot(h, wg_ref[...].astype(BF16), preferred_element_type=F32)
    up = jnp.dot(h, wu_ref[...].astype(BF16), preferred_element_type=F32)
    act = ((gate * jax.nn.sigmoid(gate)) * (0.5 * up)).astype(BF16)
    for c in range(0, d, d_chunk):
        acc_ref[:, c:c + d_chunk] += jnp.dot(act, wd_ref[:, c:c + d_chunk].astype(BF16),
                                             preferred_element_type=F32)

    rows = tm // n_norm_chunks
    r0 = pl.multiple_of(jnp.clip(j - 2, 0, n_norm_chunks - 1) * rows, rows)
    h_next_ref[pl.ds(r0, rows), :] = _rms(acc_next_ref[pl.ds(r0, rows), :], g_ref[...]).astype(BF16)

    @pl.when((j == 1) & (i + 1 < n_tiles))
    def _():
        x_next_copy().start()

    @pl.when(j == n_ff - 1)
    def _():
        if final_norm:
            acc_ref[...] = _rms(acc_ref[...], gf_ref[...])
        o_copy().start()

        @pl.when(i == n_tiles - 1)
        def _():
            o_copy().wait()


def _ffn_body(x_hbm, g_ref, wg_ref, wu_ref, wd_ref, gf_ref, o_hbm, acc0_ref, acc1_ref, h0_ref, h1_ref,
              in_sem, out_sem, **static):
    i = pl.program_id(0)
    run = functools.partial(_ffn_step, x_hbm=x_hbm, g_ref=g_ref, wg_ref=wg_ref, wu_ref=wu_ref, wd_ref=wd_ref,
                            gf_ref=gf_ref, o_hbm=o_hbm, **static)

    @pl.when(i % 2 == 0)
    def _():
        run(acc0_ref, acc1_ref, h0_ref, h1_ref, in_sem.at[0], out_sem.at[0], in_sem.at[1], out_sem.at[1])

    @pl.when(i % 2 == 1)
    def _():
        run(acc1_ref, acc0_ref, h1_ref, h0_ref, in_sem.at[1], out_sem.at[1], in_sem.at[0], out_sem.at[0])


def _ffn(x, norm_g, w_gate, w_up, w_down, norm_final, layer, *, final_norm, tm, tf):
    s, d = x.shape
    f = w_gate.shape[-1]
    tm = min(tm, s)
    tf = min(tf, f)
    n_ff = f // tf
    assert n_ff >= 3
    n_norm_chunks = 1 << ((n_ff - 2).bit_length() - 1)
    body = functools.partial(_ffn_body, n_ff=n_ff, n_norm_chunks=n_norm_chunks, d_chunk=min(512, d),
                             final_norm=final_norm)
    return pl.pallas_call(
        body,
        grid=(s // tm, n_ff),
        in_specs=[
            pl.BlockSpec(memory_space=pl.ANY),
            pl.BlockSpec((None, 1, d), lambda i, j: (layer, 0, 0)),
            pl.BlockSpec((None, d, tf), lambda i, j: (layer, 0, j)),
            pl.BlockSpec((None, d, tf), lambda i, j: (layer, 0, j)),
            pl.BlockSpec((None, tf, d), lambda i, j: (layer, j, 0)),
            pl.BlockSpec((1, d), lambda i, j: (0, 0)),
        ],
        out_specs=pl.BlockSpec(memory_space=pl.ANY),
        out_shape=jax.ShapeDtypeStruct((s, d), F32),
        scratch_shapes=[pltpu.VMEM((tm, d), F32), pltpu.VMEM((tm, d), F32),
                        pltpu.VMEM((tm, d), BF16), pltpu.VMEM((tm, d), BF16),
                        pltpu.SemaphoreType.DMA((2,)), pltpu.SemaphoreType.DMA((2,))],
        compiler_params=pltpu.CompilerParams(
            dimension_semantics=("arbitrary", "arbitrary"),
            vmem_limit_bytes=V7X_VMEM_LIMIT_BYTES),
        name="ffn",
    )(x, norm_g, w_gate, w_up, w_down, norm_final)


def _lru_coeffs(ext_ref, offsets, cw_ref, cb_ref, wax_ref, bax_ref, lam_ref, a_ref, b_ref):
    tm, c = a_ref.shape
    n_heads = wax_ref.shape[0]
    hd = c // n_heads
    xc = cb_ref[...] + cw_ref[0:1, :] * ext_ref[pl.ds(offsets[0], tm), :]
    for k in range(1, len(offsets)):
        xc = xc + cw_ref[k:k + 1, :] * ext_ref[pl.ds(offsets[k], tm), :]
    softplus_neg_lam = jax.nn.softplus(-lam_ref[...])
    for h in range(n_heads):
        sl = slice(h * hd, (h + 1) * hd)
        xh = xc[:, sl]
        ri = jnp.dot(xh.astype(BF16), wax_ref[h], preferred_element_type=F32) + bax_ref[h]
        r = jax.nn.sigmoid(ri[:, :hd])
        ig = jax.nn.sigmoid(ri[:, hd:])
        log_a = -LRU_C * r * softplus_neg_lam[:, sl]
        a = jnp.exp(log_a)
        a_ref[:, sl] = a
        b_ref[:, sl] = jnp.sqrt(1.0 - a * a) * (ig * xh)


def _sublane_scan(a, b, *, reverse):
    row = lax.broadcasted_iota(jnp.int32, a.shape, 0)
    for dist in (1, 2, 4):
        if reverse:
            keep = row < SUBLANES - dist
            shift = SUBLANES - dist
        else:
            keep = row >= dist
            shift = dist
        a_n = jnp.where(keep, pltpu.roll(a, shift, 0), 1.0)
        b_n = jnp.where(keep, pltpu.roll(b, shift, 0), 0.0)
        b = a * b_n + b
        a = a * a_n
    return a, b


def _lru_scan(a_ref, b_ref, h_ref, carry_ref, *, reverse):
    tm, c = a_ref.shape
    n_groups = tm // SUBLANES
    order = range(n_groups - 1, -1, -1) if reverse else range(n_groups)
    rows = lambda j: slice(j * SUBLANES, (j + 1) * SUBLANES)
    row = lax.broadcasted_iota(jnp.int32, (SUBLANES, c), 0)

    h = jnp.zeros((SUBLANES, c), F32)
    p = jnp.ones((SUBLANES, c), F32)
    for j in order:
        a = a_ref[rows(j), :]
        h = a * h + b_ref[rows(j), :]
        p = a * p
    a_cum, b_cum = _sublane_scan(p, h, reverse=reverse)
    carry = carry_ref[...]
    h_edge = a_cum * carry + b_cum
    if reverse:
        h = jnp.where(row == SUBLANES - 1, carry, pltpu.roll(h_edge, SUBLANES - 1, 0))
        carry_ref[...] = jnp.broadcast_to(h_edge[0:1, :], (SUBLANES, c))
    else:
        h = jnp.where(row == 0, carry, pltpu.roll(h_edge, 1, 0))
        carry_ref[...] = jnp.broadcast_to(h_edge[SUBLANES - 1:SUBLANES, :], (SUBLANES, c))
    for j in order:
        h = a_ref[rows(j), :] * h + b_ref[rows(j), :]
        h_ref[rows(j), :] = h


def _wrap_rows(dst_ref, dst_row, inner, outer, *, forward):
    row = lax.broadcasted_iota(jnp.int32, inner.shape, 0)
    if forward:
        val = jnp.where(row == SUBLANES - 1, pltpu.roll(outer, SUBLANES - 1, 0),
                        pltpu.roll(inner, SUBLANES - 1, 0))
    else:
        val = jnp.where(row == 0, pltpu.roll(outer, 1, 0), pltpu.roll(inner, 1, 0))
    dst_ref[dst_row:dst_row + SUBLANES, :] = val


def _proj_bwd_body(x_ref, g_ref, perm_ref, w_ref, cw_ref, cb_ref, wax_ref, bax_ref, lam_ref,
                   ua_ref, xr_ref, hb_ref, ext_ref, a_ref, b_ref, carry_ref, *, pool_w, conv_w):
    i = pl.program_id(0)
    tm = x_ref.shape[0]
    c = xr_ref.shape[1]
    n_in = w_ref.shape[1] - c
    n_taps = cw_ref.shape[0]
    halo = (n_taps - 1) * SUBLANES

    @pl.when(i == 0)
    def _():
        carry_ref[...] = jnp.zeros_like(carry_ref)
        ext_ref[0:halo, :] = jnp.zeros((halo, c), F32)

    h = _rms(x_ref[...], g_ref[...]).astype(BF16)
    hp = jnp.dot(perm_ref[...], h, preferred_element_type=F32).astype(BF16)
    xr = jnp.dot(hp, w_ref[:, n_in:], preferred_element_type=F32)
    u = jnp.dot(hp, w_ref[:, :n_in], preferred_element_type=F32)
    v0, g0, lg0 = pool_w, pool_w + conv_w, pool_w + 2 * conv_w
    ua_ref[:, 0:pool_w] = u[:, 0:pool_w]
    ua_ref[:, pool_w:pool_w + conv_w] = u[:, v0:v0 + conv_w] * jax.nn.sigmoid(u[:, g0:g0 + conv_w])
    ua_ref[:, pool_w + conv_w:] = jax.nn.gelu(u[:, lg0:])
    xr_ref[...] = xr
    for j in range(n_taps - 1):
        rows = slice(j * SUBLANES, (j + 1) * SUBLANES)
        _wrap_rows(ext_ref, tm + j * SUBLANES, xr[rows, :], ext_ref[rows, :], forward=True)
    ext_ref[0:tm, :] = xr
    offsets = tuple((n_taps - 1 - k) * SUBLANES for k in range(n_taps))
    _lru_coeffs(ext_ref, offsets, cw_ref, cb_ref, wax_ref, bax_ref, lam_ref, a_ref, b_ref)
    _lru_scan(a_ref, b_ref, hb_ref, carry_ref, reverse=True)


def _proj_bwd(x, norm_g, perm, w_in_bf16, cw, cb, wax, bax, lam, layer, *, tm, pool_w, conv_w):
    s, d = x.shape
    d_in = w_in_bf16.shape[-1]
    n_taps = cw.shape[2]
    n_heads, hd = wax.shape[2], wax.shape[3]
    c = n_heads * hd
    na = d_in - c - conv_w
    nt = s // tm
    rev = lambda i: (nt - 1 - i, 0)
    body = functools.partial(_proj_bwd_body, pool_w=pool_w, conv_w=conv_w)
    return pl.pallas_call(
        body,
        grid=(nt,),
        in_specs=[
            pl.BlockSpec((tm, d), rev),
            _layer_block((1, d), layer),
            pl.BlockSpec((tm, tm), lambda i: (0, 0), pipeline_mode=pl.Buffered(1)),
            _layer_block((d, d_in), layer),
            _dir_block((n_taps, c), layer, 1),
            _dir_block((1, c), layer, 1),
            _dir_block((n_heads, hd, 2 * hd), layer, 1),
            _dir_block((n_heads, 1, 2 * hd), layer, 1),
            _dir_block((1, c), layer, 1),
        ],
        out_specs=[pl.BlockSpec((tm, na), rev), pl.BlockSpec((tm, c), rev), pl.BlockSpec((tm, c), rev)],
        out_shape=[jax.ShapeDtypeStruct((s, na), F32), jax.ShapeDtypeStruct((s, c), F32),
                   jax.ShapeDtypeStruct((s, c), F32)],
        scratch_shapes=[pltpu.VMEM((tm + (n_taps - 1) * SUBLANES, c), F32), pltpu.VMEM((tm, c), F32),
                        pltpu.VMEM((tm, c), F32), pltpu.VMEM((SUBLANES, c), F32)],
        compiler_params=pltpu.CompilerParams(
            dimension_semantics=("arbitrary",), vmem_limit_bytes=V7X_VMEM_LIMIT_BYTES),
        name="proj_bwd",
    )(x, norm_g, perm, w_in_bf16, cw, cb, wax, bax, lam)


def _mix_step(y_prev_ref, y_ref, x_ref, ua_ref, uprev_ref, unext_ref, xr_ref, xrprev_ref, hb_ref,
              pw_ref, ps_ref, dww_ref, dwb_ref, lng_ref, lnb_ref,
              cw_ref, cb_ref, wax_ref, bax_ref, lam_ref, wout_ref, unperm_ref, o_ref,
              pzext_ref, xext_ref, a_ref, b_ref, hf_ref, yp_ref, carry_ref, *, seq_len, n_tiles):
    step = pl.program_id(0)
    i = jnp.minimum(step, n_tiles - 1)
    tm = ua_ref.shape[0]
    chunk = tm // SUBLANES
    first = i == 0
    last = i == n_tiles - 1
    n_groups, pg = pw_ref.shape[0], pw_ref.shape[1]
    pool_w = n_groups * pg
    conv_w = dww_ref.shape[1]
    pz_w = pool_w + conv_w
    hv = HALO_GROUPS
    rows = lambda j: slice(j * SUBLANES, (j + 1) * SUBLANES)

    o_ref[...] = x_ref[...] + jnp.dot(y_prev_ref[...], wout_ref[...], preferred_element_type=F32)

    for j in range(hv):
        before = jnp.where(first, 0.0, uprev_ref[rows(j), :])
        _wrap_rows(pzext_ref, j * SUBLANES, ua_ref[rows(chunk - hv + j), 0:pz_w], before, forward=False)
        after = jnp.where(last, 0.0, unext_ref[rows(j), :])
        _wrap_rows(pzext_ref, (hv + chunk + j) * SUBLANES, ua_ref[rows(j), 0:pz_w], after, forward=True)
    pzext_ref[hv * SUBLANES:hv * SUBLANES + tm, :] = ua_ref[:, 0:pz_w]

    pos = lax.broadcasted_iota(jnp.int32, (tm, pg), 0)
    t = i * tm + (pos & (SUBLANES - 1)) * chunk + (pos >> 3)
    for g, win in enumerate(POOL_WINDOWS):
        sl = slice(g * pg, (g + 1) * pg)
        half = win // 2
        ssum = pzext_ref[pl.ds((hv - half) * SUBLANES, tm), sl]
        for o in range(-half + 1, half):
            ssum = ssum + pzext_ref[pl.ds((hv + o) * SUBLANES, tm), sl]
        cnt = (jnp.minimum(t + half, seq_len) - jnp.maximum(t - half, 0)).astype(F32)
        pooled = ssum / cnt - ua_ref[:, sl]
        mixed = jnp.dot(pooled.astype(BF16), pw_ref[g], preferred_element_type=F32)
        yp_ref[:, sl] = (mixed * ps_ref[:, sl]).astype(BF16)

    n_taps = dww_ref.shape[0]
    pad = n_taps // 2
    z = dwb_ref[...]
    for k in range(n_taps):
        z = z + dww_ref[k:k + 1, :] * pzext_ref[pl.ds((hv - pad + k) * SUBLANES, tm), pool_w:pz_w]
    mu = jnp.mean(z, axis=-1, keepdims=True)
    zc = z - mu
    var = jnp.mean(zc * zc, axis=-1, keepdims=True)
    zn = zc * lax.rsqrt(var + LN_EPS) * lng_ref[...] + lnb_ref[...]
    yp_ref[:, pool_w:pz_w] = (zn * jax.nn.sigmoid(zn)).astype(BF16)

    lru_taps = cw_ref.shape[0]
    lh = lru_taps - 1
    n_prev = xrprev_ref.shape[0] // SUBLANES
    for j in range(lh):
        before = jnp.where(first, 0.0, xrprev_ref[rows(n_prev - lh + j), :])
        _wrap_rows(xext_ref, j * SUBLANES, xr_ref[rows(chunk - lh + j), :], before, forward=False)
    xext_ref[lh * SUBLANES:, :] = xr_ref[...]
    offsets = tuple(k * SUBLANES for k in range(lru_taps))
    _lru_coeffs(xext_ref, offsets, cw_ref, cb_ref, wax_ref, bax_ref, lam_ref, a_ref, b_ref)
    _lru_scan(a_ref, b_ref, hf_ref, carry_ref, reverse=False)
    yp_ref[:, pz_w:] = ((hf_ref[...] + hb_ref[...]) * ua_ref[:, pz_w:]).astype(BF16)

    y_ref[...] = jnp.dot(unperm_ref[...], yp_ref[...], preferred_element_type=F32).astype(BF16)


def _mix_body(*refs, seq_len, n_tiles):
    (x_ref, ua_ref, uprev_ref, unext_ref, xr_ref, xrprev_ref, hb_ref,
     pw_ref, ps_ref, dww_ref, dwb_ref, lng_ref, lnb_ref,
     cw_ref, cb_ref, wax_ref, bax_ref, lam_ref, wout_ref, unperm_ref, o_ref,
     pzext_ref, xext_ref, a_ref, b_ref, hf_ref, yp_ref, y0_ref, y1_ref, carry_ref) = refs
    step = pl.program_id(0)

    @pl.when(step == 0)
    def _():
        carry_ref[...] = jnp.zeros_like(carry_ref)
        y1_ref[...] = jnp.zeros_like(y1_ref)

    run = functools.partial(
        _mix_step, x_ref=x_ref, ua_ref=ua_ref, uprev_ref=uprev_ref, unext_ref=unext_ref, xr_ref=xr_ref,
        xrprev_ref=xrprev_ref, hb_ref=hb_ref, pw_ref=pw_ref, ps_ref=ps_ref, dww_ref=dww_ref,
        dwb_ref=dwb_ref, lng_ref=lng_ref, lnb_ref=lnb_ref, cw_ref=cw_ref, cb_ref=cb_ref, wax_ref=wax_ref,
        bax_ref=bax_ref, lam_ref=lam_ref, wout_ref=wout_ref, unperm_ref=unperm_ref, o_ref=o_ref,
        pzext_ref=pzext_ref, xext_ref=xext_ref, a_ref=a_ref, b_ref=b_ref, hf_ref=hf_ref, yp_ref=yp_ref,
        carry_ref=carry_ref, seq_len=seq_len, n_tiles=n_tiles)

    @pl.when(step % 2 == 0)
    def _():
        run(y1_ref, y0_ref)

    @pl.when(step % 2 == 1)
    def _():
        run(y0_ref, y1_ref)


def _mix(x, ua, xr, hb, unperm, pool_w, pool_scale, dw_w, dw_b, ln_g, ln_b, cw, cb, wax, bax, lam,
         w_out_bf16, layer, *, tm):
    s, d = x.shape
    na = ua.shape[1]
    c = xr.shape[1]
    nt = s // tm
    n_pool_groups, pg = pool_w.shape[1], pool_w.shape[2]
    pool_width = n_pool_groups * pg
    conv_width = dw_w.shape[2]
    conv_taps = dw_w.shape[1]
    lru_taps = cw.shape[2]
    n_heads, hd = wax.shape[2], wax.shape[3]
    d_mix = w_out_bf16.shape[1]
    pz_w = pool_width + conv_width
    halo_rows = HALO_GROUPS * SUBLANES
    lru_halo_rows = LRU_HALO_GROUPS * SUBLANES
    assert conv_taps // 2 <= HALO_GROUPS and max(POOL_WINDOWS) // 2 <= HALO_GROUPS
    assert lru_taps - 1 <= LRU_HALO_GROUPS and pz_w + c == d_mix == na
    assert tm % halo_rows == 0 and pz_w % 128 == 0
    halo_per_tile = tm // halo_rows
    lru_per_tile = tm // lru_halo_rows
    last_halo = s // halo_rows - 1

    tile = lambda st: jnp.minimum(st, nt - 1)
    cur = lambda st: (tile(st), 0)
    prev = lambda st: (jnp.maximum(st - 1, 0), 0)
    body = functools.partial(_mix_body, seq_len=s, n_tiles=nt)
    return pl.pallas_call(
        body,
        grid=(nt + 1,),
        in_specs=[
            pl.BlockSpec((tm, d), prev),
            pl.BlockSpec((tm, na), cur),
            pl.BlockSpec((halo_rows, pz_w), lambda st: (jnp.maximum(tile(st) * halo_per_tile - 1, 0), 0)),
            pl.BlockSpec((halo_rows, pz_w),
                         lambda st: (jnp.minimum((tile(st) + 1) * halo_per_tile, last_halo), 0)),
            pl.BlockSpec((tm, c), cur),
            pl.BlockSpec((lru_halo_rows, c), lambda st: (jnp.maximum(tile(st) * lru_per_tile - 1, 0), 0)),
            pl.BlockSpec((tm, c), cur),
            _layer_block((n_pool_groups, pg, pg), layer),
            _layer_block((1, pool_width), layer),
            _layer_block((conv_taps, conv_width), layer),
            _layer_block((1, conv_width), layer),
            _layer_block((1, conv_width), layer),
            _layer_block((1, conv_width), layer),
            _dir_block((lru_taps, c), layer, 0),
            _dir_block((1, c), layer, 0),
            _dir_block((n_heads, hd, 2 * hd), layer, 0),
            _dir_block((n_heads, 1, 2 * hd), layer, 0),
            _dir_block((1, c), layer, 0),
            _layer_block((d_mix, d), layer),
            pl.BlockSpec((tm, tm), lambda st: (0, 0), pipeline_mode=pl.Buffered(1)),
        ],
        out_specs=pl.BlockSpec((tm, d), prev),
        out_shape=jax.ShapeDtypeStruct((s, d), F32),
        scratch_shapes=[
            pltpu.VMEM((tm + 2 * halo_rows, pz_w), F32),
            pltpu.VMEM((tm + (lru_taps - 1) * SUBLANES, c), F32),
            pltpu.VMEM((tm, c), F32),
            pltpu.VMEM((tm, c), F32),
            pltpu.VMEM((tm, c), F32),
            pltpu.VMEM((tm, d_mix), BF16),
            pltpu.VMEM((tm, d_mix), BF16),
            pltpu.VMEM((tm, d_mix), BF16),
            pltpu.VMEM((SUBLANES, c), F32),
        ],
        compiler_params=pltpu.CompilerParams(
            dimension_semantics=("arbitrary",), vmem_limit_bytes=V7X_VMEM_LIMIT_BYTES),
        name="mix",
    )(x, ua, ua, ua, xr, xr, hb, pool_w, pool_scale, dw_w, dw_b, ln_g, ln_b,
      cw, cb, wax, bax, lam, w_out_bf16, unperm)


def _chunk_permutation(tm):
    chunk = tm // SUBLANES
    pos = jnp.arange(tm)
    src = (pos % SUBLANES) * chunk + pos // SUBLANES
    return (src[:, None] == jnp.arange(tm)[None, :]).astype(BF16)


def kernel(x, norm_ffn1, ffn1_w_gate, ffn1_w_up, ffn1_w_down, norm_mix, w_in, pool_w, pool_scale, conv_dw_w, conv_dw_b, conv_ln_g, conv_ln_b, lru_conv_w, lru_conv_b, lru_w_a, lru_b_a, lru_w_x, lru_b_x, lru_lambda, w_out, norm_ffn2, ffn2_w_gate, ffn2_w_up, ffn2_w_down, norm_final):
    b, s, d = x.shape
    depth = w_in.shape[0]
    n_dirs, n_heads, hd = lru_w_a.shape[1], lru_w_a.shape[2], lru_w_a.shape[3]

    row = lambda p: p[:, None, :]
    norm_ffn1_r, norm_mix_r, norm_ffn2_r = row(norm_ffn1), row(norm_mix), row(norm_ffn2)
    norm_final_r = norm_final[None, :]
    pool_scale_r, dw_b_r, ln_g_r, ln_b_r = row(pool_scale), row(conv_dw_b), row(conv_ln_g), row(conv_ln_b)
    lru_cb_r = lru_conv_b[:, :, None, :]
    lam_r = lru_lambda[:, :, None, :]
    wax = jnp.concatenate([lru_w_a, lru_w_x], axis=-1).astype(BF16)
    bax = jnp.concatenate([lru_b_a.reshape(depth, n_dirs, n_heads, 1, hd),
                           lru_b_x.reshape(depth, n_dirs, n_heads, 1, hd)], axis=-1)
    w_in_b = w_in.astype(BF16)
    w_out_b = w_out.astype(BF16)
    pool_w_b = pool_w.astype(BF16)
    tm_mix = min(MIX_ROWS, s)
    perm = _chunk_permutation(tm_mix)
    unperm = perm.T
    pool_width = pool_w.shape[1] * pool_w.shape[2]
    conv_width = conv_dw_w.shape[2]

    outs = []
    for bi in range(b):
        xb = x[bi]
        for l in range(depth):
            xb = _ffn(xb, norm_ffn1_r, ffn1_w_gate, ffn1_w_up, ffn1_w_down, norm_final_r, l,
                      final_norm=False, tm=FFN_ROWS, tf=FFN_COLS)
            ua, xr, hb = _proj_bwd(xb, norm_mix_r, perm, w_in_b, lru_conv_w, lru_cb_r, wax, bax, lam_r, l,
                                   tm=tm_mix, pool_w=pool_width, conv_w=conv_width)
            xb = _mix(xb, ua, xr, hb, unperm, pool_w_b, pool_scale_r, conv_dw_w, dw_b_r, ln_g_r, ln_b_r,
                      lru_conv_w, lru_cb_r, wax, bax, lam_r, w_out_b, l, tm=tm_mix)
            xb = _ffn(xb, norm_ffn2_r, ffn2_w_gate, ffn2_w_up, ffn2_w_down, norm_final_r, l,
                      final_norm=(l == depth - 1), tm=FFN_ROWS, tf=FFN_COLS)
        outs.append(xb)
    return jnp.stack(outs, axis=0) if b > 1 else outs[0][None]
```

```python
import functools

import jax
import jax.numpy as jnp
from jax import lax
from jax.experimental import pallas as pl
from jax.experimental.pallas import tpu as pltpu

RMS_EPS = 1e-6
LN_EPS = 1e-5
LRU_C = 8.0
POOL_WINDOWS = (2, 4, 8, 16)

SUBLANES = 8
HALO_GROUPS = 16
LRU_HALO_GROUPS = 4
V7X_VMEM_LIMIT_BYTES = 56 * 1024 * 1024

F32 = jnp.float32
BF16 = jnp.bfloat16

FFN_ROWS = 1024
FFN_COLS = 512
MIX_ROWS = 256


def _rms(x, g):
    return x * lax.rsqrt(jnp.mean(x * x, axis=-1, keepdims=True) + RMS_EPS) * g


def _layer_block(shape, layer):
    zeros = (0,) * len(shape)
    return pl.BlockSpec((None,) + tuple(shape), lambda *_: (layer,) + zeros, pipeline_mode=pl.Buffered(1))


def _dir_block(shape, layer, direction):
    zeros = (0,) * len(shape)
    return pl.BlockSpec((None, None) + tuple(shape), lambda *_: (layer, direction) + zeros,
                        pipeline_mode=pl.Buffered(1))


def _ffn_body(x_hbm, g_ref, wg_ref, wu_ref, wd_ref, gf_ref, o_hbm, acc_ref, h_ref, in_sem, out_sem,
              *, n_ff, d_chunk, final_norm):
    i = pl.program_id(0)
    j = pl.program_id(1)
    n_tiles = pl.num_programs(0)
    tm = acc_ref.shape[1]
    slot = i % 2
    other = 1 - slot

    def x_copy(tile, s):
        return pltpu.make_async_copy(x_hbm.at[pl.ds(tile * tm, tm)], acc_ref.at[s], in_sem.at[s])

    def o_copy(tile, s):
        return pltpu.make_async_copy(acc_ref.at[s], o_hbm.at[pl.ds(tile * tm, tm)], out_sem.at[s])

    @pl.when(j == 0)
    def _():
        @pl.when(i == 0)
        def _():
            x_copy(0, 0).start()

        x_copy(i, slot).wait()
        h_ref[...] = _rms(acc_ref[slot], g_ref[...]).astype(BF16)

    @pl.when(j == min(2, n_ff - 1))
    def _():
        @pl.when(i >= 1)
        def _():
            o_copy(i - 1, other).wait()

        @pl.when(i + 1 < n_tiles)
        def _():
            x_copy(i + 1, other).start()

    h = h_ref[...]
    gate = jnp.dot(h, wg_ref[...].astype(BF16), preferred_element_type=F32)
    up = jnp.dot(h, wu_ref[...].astype(BF16), preferred_element_type=F32)
    act = ((gate * jax.nn.sigmoid(gate)) * (0.5 * up)).astype(BF16)
    d = acc_ref.shape[2]
    for c in range(0, d, d_chunk):
        acc_ref[slot, :, c:c + d_chunk] += jnp.dot(act, wd_ref[:, c:c + d_chunk].astype(BF16),
                                                   preferred_element_type=F32)

    @pl.when(j == n_ff - 1)
    def _():
        if final_norm:
            acc_ref[slot] = _rms(acc_ref[slot], gf_ref[...])
        o_copy(i, slot).start()

        @pl.when(i == n_tiles - 1)
        def _():
            o_copy(i, slot).wait()


def _ffn(x, norm_g, w_gate, w_up, w_down, norm_final, layer, *, final_norm, tm, tf):
    s, d = x.shape
    f = w_gate.shape[-1]
    tm = min(tm, s)
    tf = min(tf, f)
    n_ff = f // tf
    assert n_ff >= 2
    body = functools.partial(_ffn_body, n_ff=n_ff, d_chunk=min(512, d), final_norm=final_norm)
    return pl.pallas_call(
        body,
        grid=(s // tm, n_ff),
        in_specs=[
            pl.BlockSpec(memory_space=pl.ANY),
            pl.BlockSpec((None, 1, d), lambda i, j: (layer, 0, 0)),
            pl.BlockSpec((None, d, tf), lambda i, j: (layer, 0, j)),
            pl.BlockSpec((None, d, tf), lambda i, j: (layer, 0, j)),
            pl.BlockSpec((None, tf, d), lambda i, j: (layer, j, 0)),
            pl.BlockSpec((1, d), lambda i, j: (0, 0)),
        ],
        out_specs=pl.BlockSpec(memory_space=pl.ANY),
        out_shape=jax.ShapeDtypeStruct((s, d), F32),
        scratch_shapes=[pltpu.VMEM((2, tm, d), F32), pltpu.VMEM((tm, d), BF16),
                        pltpu.SemaphoreType.DMA((2,)), pltpu.SemaphoreType.DMA((2,))],
        compiler_params=pltpu.CompilerParams(
            dimension_semantics=("arbitrary", "arbitrary"),
            vmem_limit_bytes=V7X_VMEM_LIMIT_BYTES),
        name="ffn",
    )(x, norm_g, w_gate, w_up, w_down, norm_final)


def _lru_coeffs(ext_ref, offsets, cw_ref, cb_ref, wax_ref, bax_ref, lam_ref, a_ref, b_ref):
    tm, c = a_ref.shape
    n_heads = wax_ref.shape[0]
    hd = c // n_heads
    xc = cb_ref[...] + cw_ref[0:1, :] * ext_ref[pl.ds(offsets[0], tm), :]
    for k in range(1, len(offsets)):
        xc = xc + cw_ref[k:k + 1, :] * ext_ref[pl.ds(offsets[k], tm), :]
    softplus_neg_lam = jax.nn.softplus(-lam_ref[...])
    for h in range(n_heads):
        sl = slice(h * hd, (h + 1) * hd)
        xh = xc[:, sl]
        ri = jnp.dot(xh.astype(BF16), wax_ref[h], preferred_element_type=F32) + bax_ref[h]
        r = jax.nn.sigmoid(ri[:, :hd])
        ig = jax.nn.sigmoid(ri[:, hd:])
        log_a = -LRU_C * r * softplus_neg_lam[:, sl]
        a = jnp.exp(log_a)
        a_ref[:, sl] = a
        v = 1.0 - a * a
        b_ref[:, sl] = jnp.where(v > 0.0, v * lax.rsqrt(v), 0.0) * (ig * xh)


def _sublane_scan(a, b, *, reverse):
    row = lax.broadcasted_iota(jnp.int32, a.shape, 0)
    for dist in (1, 2, 4):
        if reverse:
            keep = row < SUBLANES - dist
            shift = SUBLANES - dist
        else:
            keep = row >= dist
            shift = dist
        a_n = jnp.where(keep, pltpu.roll(a, shift, 0), 1.0)
        b_n = jnp.where(keep, pltpu.roll(b, shift, 0), 0.0)
        b = a * b_n + b
        a = a * a_n
    return a, b


def _lru_scan(a_ref, b_ref, h_ref, carry_ref, *, reverse):
    tm, c = a_ref.shape
    n_groups = tm // SUBLANES
    order = range(n_groups - 1, -1, -1) if reverse else range(n_groups)
    rows = lambda j: slice(j * SUBLANES, (j + 1) * SUBLANES)
    row = lax.broadcasted_iota(jnp.int32, (SUBLANES, c), 0)

    h = jnp.zeros((SUBLANES, c), F32)
    p = jnp.ones((SUBLANES, c), F32)
    for j in order:
        a = a_ref[rows(j), :]
        h = a * h + b_ref[rows(j), :]
        p = a * p
    a_cum, b_cum = _sublane_scan(p, h, reverse=reverse)
    carry = carry_ref[...]
    h_edge = a_cum * carry + b_cum
    if reverse:
        h = jnp.where(row == SUBLANES - 1, carry, pltpu.roll(h_edge, SUBLANES - 1, 0))
        carry_ref[...] = jnp.broadcast_to(h_edge[0:1, :], (SUBLANES, c))
    else:
        h = jnp.where(row == 0, carry, pltpu.roll(h_edge, 1, 0))
        carry_ref[...] = jnp.broadcast_to(h_edge[SUBLANES - 1:SUBLANES, :], (SUBLANES, c))
    for j in order:
        h = a_ref[rows(j), :] * h + b_ref[rows(j), :]
        h_ref[rows(j), :] = h


def _wrap_rows(dst_ref, dst_row, inner, outer, *, forward):
    row = lax.broadcasted_iota(jnp.int32, inner.shape, 0)
    if forward:
        val = jnp.where(row == SUBLANES - 1, pltpu.roll(outer, SUBLANES - 1, 0),
                        pltpu.roll(inner, SUBLANES - 1, 0))
    else:
        val = jnp.where(row == 0, pltpu.roll(outer, 1, 0), pltpu.roll(inner, 1, 0))
    dst_ref[dst_row:dst_row + SUBLANES, :] = val


def _proj_bwd_body(x_ref, g_ref, perm_ref, w_ref, cw_ref, cb_ref, wax_ref, bax_ref, lam_ref,
                   ua_ref, xr_ref, hb_ref, ext_ref, a_ref, b_ref, carry_ref, *, pool_w, conv_w):
    i = pl.program_id(0)
    tm = x_ref.shape[0]
    c = xr_ref.shape[1]
    n_in = w_ref.shape[1] - c
    n_taps = cw_ref.shape[0]
    halo = (n_taps - 1) * SUBLANES

    @pl.when(i == 0)
    def _():
        carry_ref[...] = jnp.zeros_like(carry_ref)
        ext_ref[0:halo, :] = jnp.zeros((halo, c), F32)

    h = _rms(x_ref[...], g_ref[...]).astype(BF16)
    hp = jnp.dot(perm_ref[...], h, preferred_element_type=F32).astype(BF16)
    xr = jnp.dot(hp, w_ref[:, n_in:], preferred_element_type=F32)
    u = jnp.dot(hp, w_ref[:, :n_in], preferred_element_type=F32)
    v0, g0, lg0 = pool_w, pool_w + conv_w, pool_w + 2 * conv_w
    ua_ref[:, 0:pool_w] = u[:, 0:pool_w]
    ua_ref[:, pool_w:pool_w + conv_w] = u[:, v0:v0 + conv_w] * jax.nn.sigmoid(u[:, g0:g0 + conv_w])
    ua_ref[:, pool_w + conv_w:] = jax.nn.gelu(u[:, lg0:])
    xr_ref[...] = xr
    for j in range(n_taps - 1):
        rows = slice(j * SUBLANES, (j + 1) * SUBLANES)
        _wrap_rows(ext_ref, tm + j * SUBLANES, xr[rows, :], ext_ref[rows, :], forward=True)
    ext_ref[0:tm, :] = xr
    offsets = tuple((n_taps - 1 - k) * SUBLANES for k in range(n_taps))
    _lru_coeffs(ext_ref, offsets, cw_ref, cb_ref, wax_ref, bax_ref, lam_ref, a_ref, b_ref)
    _lru_scan(a_ref, b_ref, hb_ref, carry_ref, reverse=True)


def _proj_bwd(x, norm_g, perm, w_in_bf16, cw, cb, wax, bax, lam, layer, *, tm, pool_w, conv_w):
    s, d = x.shape
    d_in = w_in_bf16.shape[-1]
    n_taps = cw.shape[2]
    n_heads, hd = wax.shape[2], wax.shape[3]
    c = n_heads * hd
    na = d_in - c - conv_w
    nt = s // tm
    rev = lambda i: (nt - 1 - i, 0)
    body = functools.partial(_proj_bwd_body, pool_w=pool_w, conv_w=conv_w)
    return pl.pallas_call(
        body,
        grid=(nt,),
        in_specs=[
            pl.BlockSpec((tm, d), rev),
            _layer_block((1, d), layer),
            pl.BlockSpec((tm, tm), lambda i: (0, 0), pipeline_mode=pl.Buffered(1)),
            _layer_block((d, d_in), layer),
            _dir_block((n_taps, c), layer, 1),
            _dir_block((1, c), layer, 1),
            _dir_block((n_heads, hd, 2 * hd), layer, 1),
            _dir_block((n_heads, 1, 2 * hd), layer, 1),
            _dir_block((1, c), layer, 1),
        ],
        out_specs=[pl.BlockSpec((tm, na), rev), pl.BlockSpec((tm, c), rev), pl.BlockSpec((tm, c), rev)],
        out_shape=[jax.ShapeDtypeStruct((s, na), F32), jax.ShapeDtypeStruct((s, c), F32),
                   jax.ShapeDtypeStruct((s, c), F32)],
        scratch_shapes=[pltpu.VMEM((tm + (n_taps - 1) * SUBLANES, c), F32), pltpu.VMEM((tm, c), F32),
                        pltpu.VMEM((tm, c), F32), pltpu.VMEM((SUBLANES, c), F32)],
        compiler_params=pltpu.CompilerParams(
            dimension_semantics=("arbitrary",), vmem_limit_bytes=V7X_VMEM_LIMIT_BYTES),
        name="proj_bwd",
    )(x, norm_g, perm, w_in_bf16, cw, cb, wax, bax, lam)


def _mix_step(y_prev_ref, y_ref, x_ref, ua_ref, uprev_ref, unext_ref, xr_ref, xrprev_ref, hb_ref,
              pw_ref, ps_ref, dww_ref, dwb_ref, lng_ref, lnb_ref,
              cw_ref, cb_ref, wax_ref, bax_ref, lam_ref, wout_ref, unperm_ref, o_ref,
              pzext_ref, xext_ref, a_ref, b_ref, hf_ref, yp_ref, carry_ref, *, seq_len, n_tiles):
    step = pl.program_id(0)
    i = jnp.minimum(step, n_tiles - 1)
    tm = ua_ref.shape[0]
    chunk = tm // SUBLANES
    first = i == 0
    last = i == n_tiles - 1
    n_groups, pg = pw_ref.shape[0], pw_ref.shape[1]
    pool_w = n_groups * pg
    conv_w = dww_ref.shape[1]
    pz_w = pool_w + conv_w
    hv = HALO_GROUPS
    rows = lambda j: slice(j * SUBLANES, (j + 1) * SUBLANES)

    o_ref[...] = x_ref[...] + jnp.dot(y_prev_ref[...], wout_ref[...], preferred_element_type=F32)

    for j in range(hv):
        before = jnp.where(first, 0.0, uprev_ref[rows(j), :])
        _wrap_rows(pzext_ref, j * SUBLANES, ua_ref[rows(chunk - hv + j), 0:pz_w], before, forward=False)
        after = jnp.where(last, 0.0, unext_ref[rows(j), :])
        _wrap_rows(pzext_ref, (hv + chunk + j) * SUBLANES, ua_ref[rows(j), 0:pz_w], after, forward=True)
    pzext_ref[hv * SUBLANES:hv * SUBLANES + tm, :] = ua_ref[:, 0:pz_w]

    pos = lax.broadcasted_iota(jnp.int32, (tm, pg), 0)
    t = i * tm + (pos & (SUBLANES - 1)) * chunk + (pos >> 3)
    for g, win in enumerate(POOL_WINDOWS):
        sl = slice(g * pg, (g + 1) * pg)
        half = win // 2
        ssum = pzext_ref[pl.ds((hv - half) * SUBLANES, tm), sl]
        for o in range(-half + 1, half):
            ssum = ssum + pzext_ref[pl.ds((hv + o) * SUBLANES, tm), sl]
        cnt = (jnp.minimum(t + half, seq_len) - jnp.maximum(t - half, 0)).astype(F32)
        pooled = ssum / cnt - ua_ref[:, sl]
        mixed = jnp.dot(pooled.astype(BF16), pw_ref[g], preferred_element_type=F32)
        yp_ref[:, sl] = (mixed * ps_ref[:, sl]).astype(BF16)

    n_taps = dww_ref.shape[0]
    pad = n_taps // 2
    z = dwb_ref[...]
    for k in range(n_taps):
        z = z + dww_ref[k:k + 1, :] * pzext_ref[pl.ds((hv - pad + k) * SUBLANES, tm), pool_w:pz_w]
    mu = jnp.mean(z, axis=-1, keepdims=True)
    zc = z - mu
    var = jnp.mean(zc * zc, axis=-1, keepdims=True)
    zn = zc * lax.rsqrt(var + LN_EPS) * lng_ref[...] + lnb_ref[...]
    yp_ref[:, pool_w:pz_w] = (zn * jax.nn.sigmoid(zn)).astype(BF16)

    lru_taps = cw_ref.shape[0]
    lh = lru_taps - 1
    n_prev = xrprev_ref.shape[0] // SUBLANES
    for j in range(lh):
        before = jnp.where(first, 0.0, xrprev_ref[rows(n_prev - lh + j), :])
        _wrap_rows(xext_ref, j * SUBLANES, xr_ref[rows(chunk - lh + j), :], before, forward=False)
    xext_ref[lh * SUBLANES:, :] = xr_ref[...]
    offsets = tuple(k * SUBLANES for k in range(lru_taps))
    _lru_coeffs(xext_ref, offsets, cw_ref, cb_ref, wax_ref, bax_ref, lam_ref, a_ref, b_ref)
    _lru_scan(a_ref, b_ref, hf_ref, carry_ref, reverse=False)
    yp_ref[:, pz_w:] = ((hf_ref[...] + hb_ref[...]) * ua_ref[:, pz_w:]).astype(BF16)

    y_ref[...] = jnp.dot(unperm_ref[...], yp_ref[...], preferred_element_type=F32).astype(BF16)


def _mix_body(*refs, seq_len, n_tiles):
    (x_ref, ua_ref, uprev_ref, unext_ref, xr_ref, xrprev_ref, hb_ref,
     pw_ref, ps_ref, dww_ref, dwb_ref, lng_ref, lnb_ref,
     cw_ref, cb_ref, wax_ref, bax_ref, lam_ref, wout_ref, unperm_ref, o_ref,
     pzext_ref, xext_ref, a_ref, b_ref, hf_ref, yp_ref, y0_ref, y1_ref, carry_ref) = refs
    step = pl.program_id(0)

    @pl.when(step == 0)
    def _():
        carry_ref[...] = jnp.zeros_like(carry_ref)
        y1_ref[...] = jnp.zeros_like(y1_ref)

    run = functools.partial(
        _mix_step, x_ref=x_ref, ua_ref=ua_ref, uprev_ref=uprev_ref, unext_ref=unext_ref, xr_ref=xr_ref,
        xrprev_ref=xrprev_ref, hb_ref=hb_ref, pw_ref=pw_ref, ps_ref=ps_ref, dww_ref=dww_ref,
        dwb_ref=dwb_ref, lng_ref=lng_ref, lnb_ref=lnb_ref, cw_ref=cw_ref, cb_ref=cb_ref, wax_ref=wax_ref,
        bax_ref=bax_ref, lam_ref=lam_ref, wout_ref=wout_ref, unperm_ref=unperm_ref, o_ref=o_ref,
        pzext_ref=pzext_ref, xext_ref=xext_ref, a_ref=a_ref, b_ref=b_ref, hf_ref=hf_ref, yp_ref=yp_ref,
        carry_ref=carry_ref, seq_len=seq_len, n_tiles=n_tiles)

    @pl.when(step % 2 == 0)
    def _():
        run(y1_ref, y0_ref)

    @pl.when(step % 2 == 1)
    def _():
        run(y0_ref, y1_ref)


def _mix(x, ua, xr, hb, unperm, pool_w, pool_scale, dw_w, dw_b, ln_g, ln_b, cw, cb, wax, bax, lam,
         w_out_bf16, layer, *, tm):
    s, d = x.shape
    na = ua.shape[1]
    c = xr.shape[1]
    nt = s // tm
    n_pool_groups, pg = pool_w.shape[1], pool_w.shape[2]
    pool_width = n_pool_groups * pg
    conv_width = dw_w.shape[2]
    conv_taps = dw_w.shape[1]
    lru_taps = cw.shape[2]
    n_heads, hd = wax.shape[2], wax.shape[3]
    d_mix = w_out_bf16.shape[1]
    pz_w = pool_width + conv_width
    halo_rows = HALO_GROUPS * SUBLANES
    lru_halo_rows = LRU_HALO_GROUPS * SUBLANES
    assert conv_taps // 2 <= HALO_GROUPS and max(POOL_WINDOWS) // 2 <= HALO_GROUPS
    assert lru_taps - 1 <= LRU_HALO_GROUPS and pz_w + c == d_mix == na
    assert tm % halo_rows == 0 and pz_w % 128 == 0
    halo_per_tile = tm // halo_rows
    lru_per_tile = tm // lru_halo_rows
    last_halo = s // halo_rows - 1

    tile = lambda st: jnp.minimum(st, nt - 1)
    cur = lambda st: (tile(st), 0)
    prev = lambda st: (jnp.maximum(st - 1, 0), 0)
    body = functools.partial(_mix_body, seq_len=s, n_tiles=nt)
    return pl.pallas_call(
        body,
        grid=(nt + 1,),
        in_specs=[
            pl.BlockSpec((tm, d), prev),
            pl.BlockSpec((tm, na), cur),
            pl.BlockSpec((halo_rows, pz_w), lambda st: (jnp.maximum(tile(st) * halo_per_tile - 1, 0), 0)),
            pl.BlockSpec((halo_rows, pz_w),
                         lambda st: (jnp.minimum((tile(st) + 1) * halo_per_tile, last_halo), 0)),
            pl.BlockSpec((tm, c), cur),
            pl.BlockSpec((lru_halo_rows, c), lambda st: (jnp.maximum(tile(st) * lru_per_tile - 1, 0), 0)),
            pl.BlockSpec((tm, c), cur),
            _layer_block((n_pool_groups, pg, pg), layer),
            _layer_block((1, pool_width), layer),
            _layer_block((conv_taps, conv_width), layer),
            _layer_block((1, conv_width), layer),
            _layer_block((1, conv_width), layer),
            _layer_block((1, conv_width), layer),
            _dir_block((lru_taps, c), layer, 0),
            _dir_block((1, c), layer, 0),
            _dir_block((n_heads, hd, 2 * hd), layer, 0),
            _dir_block((n_heads, 1, 2 * hd), layer, 0),
            _dir_block((1, c), layer, 0),
            _layer_block((d_mix, d), layer),
            pl.BlockSpec((tm, tm), lambda st: (0, 0), pipeline_mode=pl.Buffered(1)),
        ],
        out_specs=pl.BlockSpec((tm, d), prev),
        out_shape=jax.ShapeDtypeStruct((s, d), F32),
        scratch_shapes=[
            pltpu.VMEM((tm + 2 * halo_rows, pz_w), F32),
            pltpu.VMEM((tm + (lru_taps - 1) * SUBLANES, c), F32),
            pltpu.VMEM((tm, c), F32),
            pltpu.VMEM((tm, c), F32),
            pltpu.VMEM((tm, c), F32),
            pltpu.VMEM((tm, d_mix), BF16),
            pltpu.VMEM((tm, d_mix), BF16),
            pltpu.VMEM((tm, d_mix), BF16),
            pltpu.VMEM((SUBLANES, c), F32),
        ],
        compiler_params=pltpu.CompilerParams(
            dimension_semantics=("arbitrary",), vmem_limit_bytes=V7X_VMEM_LIMIT_BYTES),
        name="mix",
    )(x, ua, ua, ua, xr, xr, hb, pool_w, pool_scale, dw_w, dw_b, ln_g, ln_b,
      cw, cb, wax, bax, lam, w_out_bf16, unperm)


def _chunk_permutation(tm):
    chunk = tm // SUBLANES
    pos = jnp.arange(tm)
    src = (pos % SUBLANES) * chunk + pos // SUBLANES
    return (src[:, None] == jnp.arange(tm)[None, :]).astype(BF16)


def kernel(x, norm_ffn1, ffn1_w_gate, ffn1_w_up, ffn1_w_down, norm_mix, w_in, pool_w, pool_scale, conv_dw_w, conv_dw_b, conv_ln_g, conv_ln_b, lru_conv_w, lru_conv_b, lru_w_a, lru_b_a, lru_w_x, lru_b_x, lru_lambda, w_out, norm_ffn2, ffn2_w_gate, ffn2_w_up, ffn2_w_down, norm_final):
    b, s, d = x.shape
    depth = w_in.shape[0]
    n_dirs, n_heads, hd = lru_w_a.shape[1], lru_w_a.shape[2], lru_w_a.shape[3]

    row = lambda p: p[:, None, :]
    norm_ffn1_r, norm_mix_r, norm_ffn2_r = row(norm_ffn1), row(norm_mix), row(norm_ffn2)
    norm_final_r = norm_final[None, :]
    pool_scale_r, dw_b_r, ln_g_r, ln_b_r = row(pool_scale), row(conv_dw_b), row(conv_ln_g), row(conv_ln_b)
    lru_cb_r = lru_conv_b[:, :, None, :]
    lam_r = lru_lambda[:, :, None, :]
    wax = jnp.concatenate([lru_w_a, lru_w_x], axis=-1).astype(BF16)
    bax = jnp.concatenate([lru_b_a.reshape(depth, n_dirs, n_heads, 1, hd),
                           lru_b_x.reshape(depth, n_dirs, n_heads, 1, hd)], axis=-1)
    w_in_b = w_in.astype(BF16)
    w_out_b = w_out.astype(BF16)
    pool_w_b = pool_w.astype(BF16)
    tm_mix = min(MIX_ROWS, s)
    perm = _chunk_permutation(tm_mix)
    unperm = perm.T
    pool_width = pool_w.shape[1] * pool_w.shape[2]
    conv_width = conv_dw_w.shape[2]

    outs = []
    for bi in range(b):
        xb = x[bi]
        for l in range(depth):
            xb = _ffn(xb, norm_ffn1_r, ffn1_w_gate, ffn1_w_up, ffn1_w_down, norm_final_r, l,
                      final_norm=False, tm=FFN_ROWS, tf=FFN_COLS)
            ua, xr, hb = _proj_bwd(xb, norm_mix_r, perm, w_in_b, lru_conv_w, lru_cb_r, wax, bax, lam_r, l,
                                   tm=tm_mix, pool_w=pool_width, conv_w=conv_width)
            xb = _mix(xb, ua, xr, hb, unperm, pool_w_b, pool_scale_r, conv_dw_w, dw_b_r, ln_g_r, ln_b_r,
                      lru_conv_w, lru_cb_r, wax, bax, lam_r, w_out_b, l, tm=tm_mix)
            xb = _ffn(xb, norm_ffn2_r, ffn2_w_gate, ffn2_w_up, ffn2_w_down, norm_final_r, l,
                      final_norm=(l == depth - 1), tm=FFN_ROWS, tf=FFN_COLS)
        outs.append(xb)
    return jnp.stack(outs, axis=0) if b > 1 else outs[0][None]
```

```python
import functools

import jax
import jax.numpy as jnp
from jax import lax
from jax.experimental import pallas as pl
from jax.experimental.pallas import tpu as pltpu

RMS_EPS = 1e-6
LN_EPS = 1e-5
LRU_C = 8.0
POOL_WINDOWS = (2, 4, 8, 16)

SUBLANES = 8
HALO_GROUPS = 16
V7X_VMEM_LIMIT_BYTES = 56 * 1024 * 1024

F32 = jnp.float32
BF16 = jnp.bfloat16

FFN_ROWS = 1024
FFN_COLS = 512
MIX_ROWS = 256
PROJ_SUBTILES = 2


def _rms(x, g):
    return x * lax.rsqrt(jnp.mean(x * x, axis=-1, keepdims=True) + RMS_EPS) * g


def _layer_block(shape, layer):
    zeros = (0,) * len(shape)
    return pl.BlockSpec((None,) + tuple(shape), lambda *_: (layer,) + zeros, pipeline_mode=pl.Buffered(1))


def _dir_block(shape, layer, direction):
    zeros = (0,) * len(shape)
    return pl.BlockSpec((None, None) + tuple(shape), lambda *_: (layer, direction) + zeros,
                        pipeline_mode=pl.Buffered(1))


def _ffn_body(x_hbm, g_ref, wg_ref, wu_ref, wd_ref, gf_ref, o_hbm, acc_ref, h_ref, in_sem, out_sem,
              *, n_ff, d_chunk, final_norm):
    i = pl.program_id(0)
    j = pl.program_id(1)
    n_tiles = pl.num_programs(0)
    tm = acc_ref.shape[1]
    slot = i % 2
    other = 1 - slot

    def x_copy(tile, s):
        return pltpu.make_async_copy(x_hbm.at[pl.ds(tile * tm, tm)], acc_ref.at[s], in_sem.at[s])

    def o_copy(tile, s):
        return pltpu.make_async_copy(acc_ref.at[s], o_hbm.at[pl.ds(tile * tm, tm)], out_sem.at[s])

    @pl.when(j == 0)
    def _():
        @pl.when(i == 0)
        def _():
            x_copy(0, 0).start()

        x_copy(i, slot).wait()
        h_ref[...] = _rms(acc_ref[slot], g_ref[...]).astype(BF16)

    @pl.when(j == min(2, n_ff - 1))
    def _():
        @pl.when(i >= 1)
        def _():
            o_copy(i - 1, other).wait()

        @pl.when(i + 1 < n_tiles)
        def _():
            x_copy(i + 1, other).start()

    h = h_ref[...]
    gate = jnp.dot(h, wg_ref[...].astype(BF16), preferred_element_type=F32)
    up = jnp.dot(h, wu_ref[...].astype(BF16), preferred_element_type=F32)
    act = ((gate * jax.nn.sigmoid(gate)) * (0.5 * up)).astype(BF16)
    d = acc_ref.shape[2]
    for c in range(0, d, d_chunk):
        acc_ref[slot, :, c:c + d_chunk] += jnp.dot(act, wd_ref[:, c:c + d_chunk].astype(BF16),
                                                   preferred_element_type=F32)

    @pl.when(j == n_ff - 1)
    def _():
        if final_norm:
            acc_ref[slot] = _rms(acc_ref[slot], gf_ref[...])
        o_copy(i, slot).start()

        @pl.when(i == n_tiles - 1)
        def _():
            o_copy(i, slot).wait()


def _ffn(x, norm_g, w_gate, w_up, w_down, norm_final, layer, *, final_norm, tm, tf):
    s, d = x.shape
    f = w_gate.shape[-1]
    tm = min(tm, s)
    tf = min(tf, f)
    n_ff = f // tf
    assert n_ff >= 2
    body = functools.partial(_ffn_body, n_ff=n_ff, d_chunk=min(512, d), final_norm=final_norm)
    return pl.pallas_call(
        body,
        grid=(s // tm, n_ff),
        in_specs=[
            pl.BlockSpec(memory_space=pl.ANY),
            pl.BlockSpec((None, 1, d), lambda i, j: (layer, 0, 0)),
            pl.BlockSpec((None, d, tf), lambda i, j: (layer, 0, j)),
            pl.BlockSpec((None, d, tf), lambda i, j: (layer, 0, j)),
            pl.BlockSpec((None, tf, d), lambda i, j: (layer, j, 0)),
            pl.BlockSpec((1, d), lambda i, j: (0, 0)),
        ],
        out_specs=pl.BlockSpec(memory_space=pl.ANY),
        out_shape=jax.ShapeDtypeStruct((s, d), F32),
        scratch_shapes=[pltpu.VMEM((2, tm, d), F32), pltpu.VMEM((tm, d), BF16),
                        pltpu.SemaphoreType.DMA((2,)), pltpu.SemaphoreType.DMA((2,))],
        compiler_params=pltpu.CompilerParams(
            dimension_semantics=("arbitrary", "arbitrary"),
            vmem_limit_bytes=V7X_VMEM_LIMIT_BYTES),
        name="ffn",
    )(x, norm_g, w_gate, w_up, w_down, norm_final)


def _lru_coeffs(ext_ref, offsets, cw_ref, cb_ref, wax_ref, bax_ref, lam_ref, a_ref, b_ref):
    tm, c = a_ref.shape
    n_heads = wax_ref.shape[0]
    hd = c // n_heads
    xc = cb_ref[...] + cw_ref[0:1, :] * ext_ref[pl.ds(offsets[0], tm), :]
    for k in range(1, len(offsets)):
        xc = xc + cw_ref[k:k + 1, :] * ext_ref[pl.ds(offsets[k], tm), :]
    softplus_neg_lam = jax.nn.softplus(-lam_ref[...])
    for h in range(n_heads):
        sl = slice(h * hd, (h + 1) * hd)
        xh = xc[:, sl]
        ri = jnp.dot(xh.astype(BF16), wax_ref[h], preferred_element_type=F32) + bax_ref[h]
        r = jax.nn.sigmoid(ri[:, :hd])
        ig = jax.nn.sigmoid(ri[:, hd:])
        log_a = -LRU_C * r * softplus_neg_lam[:, sl]
        a = jnp.exp(log_a)
        a_ref[:, sl] = a
        v = 1.0 - a * a
        b_ref[:, sl] = jnp.where(v > 0.0, v * lax.rsqrt(v), 0.0) * (ig * xh)


def _sublane_scan(a, b, *, reverse):
    row = lax.broadcasted_iota(jnp.int32, a.shape, 0)
    for dist in (1, 2, 4):
        if reverse:
            keep = row < SUBLANES - dist
            shift = SUBLANES - dist
        else:
            keep = row >= dist
            shift = dist
        a_n = jnp.where(keep, pltpu.roll(a, shift, 0), 1.0)
        b_n = jnp.where(keep, pltpu.roll(b, shift, 0), 0.0)
        b = a * b_n + b
        a = a * a_n
    return a, b


def _lru_scan(a_ref, b_ref, h_ref, carry_ref, *, reverse):
    tm, c = a_ref.shape
    n_groups = tm // SUBLANES
    order = range(n_groups - 1, -1, -1) if reverse else range(n_groups)
    rows = lambda j: slice(j * SUBLANES, (j + 1) * SUBLANES)
    row = lax.broadcasted_iota(jnp.int32, (SUBLANES, c), 0)

    h = jnp.zeros((SUBLANES, c), F32)
    p = jnp.ones((SUBLANES, c), F32)
    for j in order:
        a = a_ref[rows(j), :]
        h = a * h + b_ref[rows(j), :]
        p = a * p
    a_cum, b_cum = _sublane_scan(p, h, reverse=reverse)
    carry = carry_ref[...]
    h_edge = a_cum * carry + b_cum
    if reverse:
        h = jnp.where(row == SUBLANES - 1, carry, pltpu.roll(h_edge, SUBLANES - 1, 0))
        carry_ref[...] = jnp.broadcast_to(h_edge[0:1, :], (SUBLANES, c))
    else:
        h = jnp.where(row == 0, carry, pltpu.roll(h_edge, 1, 0))
        carry_ref[...] = jnp.broadcast_to(h_edge[SUBLANES - 1:SUBLANES, :], (SUBLANES, c))
    for j in order:
        h = a_ref[rows(j), :] * h + b_ref[rows(j), :]
        h_ref[rows(j), :] = h


def _wrap_rows(dst_ref, dst_row, inner, outer, *, forward):
    row = lax.broadcasted_iota(jnp.int32, inner.shape, 0)
    if forward:
        val = jnp.where(row == SUBLANES - 1, pltpu.roll(outer, SUBLANES - 1, 0),
                        pltpu.roll(inner, SUBLANES - 1, 0))
    else:
        val = jnp.where(row == 0, pltpu.roll(outer, 1, 0), pltpu.roll(inner, 1, 0))
    dst_ref[dst_row:dst_row + SUBLANES, :] = val


def _proj_bwd_body(*refs, pool_w, conv_w, tm):
    n_sub = PROJ_SUBTILES
    (x_ref, g_ref, perm_ref, w_ref, cw_ref, cb_ref, wax_ref, bax_ref, lam_ref,
     u_ref, lhs_ref) = refs[:11]
    ext_refs = refs[11:11 + n_sub]
    a_refs = refs[11 + n_sub:11 + 2 * n_sub]
    b_refs = refs[11 + 2 * n_sub:11 + 3 * n_sub]
    carry_ref = refs[11 + 3 * n_sub]
    i = pl.program_id(0)
    c = cw_ref.shape[1]
    n_in = w_ref.shape[1] - c
    n_a = n_in - conv_w
    ua_ref = u_ref.at[:, 0:n_a]
    xr_ref = u_ref.at[:, n_a:n_a + c]
    n_taps = cw_ref.shape[0]
    halo = (n_taps - 1) * SUBLANES
    pz_w = pool_w + conv_w
    v0, lg0 = pool_w, pool_w + 2 * conv_w

    @pl.when(i == 0)
    def _():
        carry_ref[...] = jnp.zeros_like(carry_ref)
        ext_refs[0][0:halo, :] = jnp.zeros((halo, c), F32)

    h = _rms(x_ref[...], g_ref[...]).astype(BF16)
    for sub in range(n_sub):
        r = slice(sub * tm, (sub + 1) * tm)
        lhs_ref[r, :] = jnp.dot(perm_ref[...], h[r, :], preferred_element_type=F32).astype(BF16)
    xr = jnp.dot(lhs_ref[...], w_ref[:, n_in:], preferred_element_type=F32)
    xr_ref[...] = xr
    ua_ref[:, pz_w:] = jax.nn.gelu(jnp.dot(lhs_ref[...], w_ref[:, lg0:n_in], preferred_element_type=F32))
    vg = jnp.dot(lhs_ref[...], w_ref[:, v0:lg0], preferred_element_type=F32)
    ua_ref[:, pool_w:pz_w] = vg[:, :conv_w] * jax.nn.sigmoid(vg[:, conv_w:])
    ua_ref[:, 0:pool_w] = jnp.dot(lhs_ref[...], w_ref[:, 0:pool_w], preferred_element_type=F32)

    offsets = tuple((n_taps - 1 - k) * SUBLANES for k in range(n_taps))
    for sub in reversed(range(n_sub)):
        r0 = sub * tm
        ext_ref = ext_refs[sub]
        after_ref = ext_refs[(sub + 1) % n_sub]
        for j in range(n_taps - 1):
            rows = slice(j * SUBLANES, (j + 1) * SUBLANES)
            _wrap_rows(ext_ref, tm + j * SUBLANES, xr[r0 + j * SUBLANES:r0 + (j + 1) * SUBLANES, :],
                       after_ref[rows, :], forward=True)
        ext_ref[0:tm, :] = xr[r0:r0 + tm, :]
        _lru_coeffs(ext_ref, offsets, cw_ref, cb_ref, wax_ref, bax_ref, lam_ref, a_refs[sub], b_refs[sub])
        _lru_scan(a_refs[sub], b_refs[sub], u_ref.at[r0:r0 + tm, n_a + c:], carry_ref, reverse=True)


def _proj_bwd(x, norm_g, perm, w_in_bf16, cw, cb, wax, bax, lam, layer, *, tm, pool_w, conv_w):
    s, d = x.shape
    d_in = w_in_bf16.shape[-1]
    n_taps = cw.shape[2]
    n_heads, hd = wax.shape[2], wax.shape[3]
    c = n_heads * hd
    na = d_in - c - conv_w
    n_sub = PROJ_SUBTILES
    rows_blk = n_sub * tm
    assert s % rows_blk == 0
    nb = s // rows_blk
    halo = (n_taps - 1) * SUBLANES
    rev = lambda i: (nb - 1 - i, 0)
    body = functools.partial(_proj_bwd_body, pool_w=pool_w, conv_w=conv_w, tm=tm)
    return pl.pallas_call(
        body,
        grid=(nb,),
        in_specs=[
            pl.BlockSpec((rows_blk, d), rev),
            _layer_block((1, d), layer),
            pl.BlockSpec((tm, tm), lambda i: (0, 0), pipeline_mode=pl.Buffered(1)),
            _layer_block((d, d_in), layer),
            _dir_block((n_taps, c), layer, 1),
            _dir_block((1, c), layer, 1),
            _dir_block((n_heads, hd, 2 * hd), layer, 1),
            _dir_block((n_heads, 1, 2 * hd), layer, 1),
            _dir_block((1, c), layer, 1),
        ],
        out_specs=pl.BlockSpec((rows_blk, na + 2 * c), rev),
        out_shape=jax.ShapeDtypeStruct((s, na + 2 * c), F32),
        scratch_shapes=([pltpu.VMEM((rows_blk, d), BF16)]
                        + [pltpu.VMEM((tm + halo, c), F32)] * n_sub
                        + [pltpu.VMEM((tm, c), F32)] * (2 * n_sub)
                        + [pltpu.VMEM((SUBLANES, c), F32)]),
        compiler_params=pltpu.CompilerParams(
            dimension_semantics=("arbitrary",), vmem_limit_bytes=V7X_VMEM_LIMIT_BYTES),
        name="proj_bwd",
    )(x, norm_g, perm, w_in_bf16, cw, cb, wax, bax, lam)


def _mix_step(y_prev_ref, y_ref, x_ref, u_ref, unext_ref,
              pw_ref, ps_ref, dww_ref, dwb_ref, lng_ref, lnb_ref,
              cw_ref, cb_ref, wax_ref, bax_ref, lam_ref, wout_ref, unperm_ref, o_ref,
              pzext_ref, pztail_ref, xext_ref, xrtail_ref, a_ref, b_ref, hf_ref, yp_ref, carry_ref,
              *, seq_len, n_tiles):
    step = pl.program_id(0)
    i = jnp.minimum(step, n_tiles - 1)
    tm = u_ref.shape[0]
    chunk = tm // SUBLANES
    first = i == 0
    last = i == n_tiles - 1
    n_groups, pg = pw_ref.shape[0], pw_ref.shape[1]
    pool_w = n_groups * pg
    conv_w = dww_ref.shape[1]
    pz_w = pool_w + conv_w
    c = cw_ref.shape[1]
    n_a = u_ref.shape[1] - 2 * c
    ua_ref = u_ref.at[:, 0:n_a]
    xr_ref = u_ref.at[:, n_a:n_a + c]
    hb_ref = u_ref.at[:, n_a + c:]
    hv = HALO_GROUPS
    rows = lambda j: slice(j * SUBLANES, (j + 1) * SUBLANES)

    o_ref[...] = x_ref[...] + jnp.dot(y_prev_ref[...], wout_ref[...], preferred_element_type=F32)

    for j in range(hv):
        main_tail = ua_ref[rows(chunk - hv + j), 0:pz_w]
        before = jnp.where(first, 0.0, pztail_ref[rows(j), :])
        _wrap_rows(pzext_ref, j * SUBLANES, main_tail, before, forward=False)
        pztail_ref[rows(j), :] = main_tail
        after = jnp.where(last, 0.0, unext_ref[rows(j), :])
        _wrap_rows(pzext_ref, (hv + chunk + j) * SUBLANES, ua_ref[rows(j), 0:pz_w], after, forward=True)
    pzext_ref[hv * SUBLANES:hv * SUBLANES + tm, :] = ua_ref[:, 0:pz_w]

    pos = lax.broadcasted_iota(jnp.int32, (tm, pg), 0)
    t = i * tm + (pos & (SUBLANES - 1)) * chunk + (pos >> 3)
    for g, win in enumerate(POOL_WINDOWS):
        sl = slice(g * pg, (g + 1) * pg)
        half = win // 2
        ssum = pzext_ref[pl.ds((hv - half) * SUBLANES, tm), sl]
        for o in range(-half + 1, half):
            ssum = ssum + pzext_ref[pl.ds((hv + o) * SUBLANES, tm), sl]
        cnt = (jnp.minimum(t + half, seq_len) - jnp.maximum(t - half, 0)).astype(F32)
        pooled = ssum / cnt - ua_ref[:, sl]
        mixed = jnp.dot(pooled.astype(BF16), pw_ref[g], preferred_element_type=F32)
        yp_ref[:, sl] = (mixed * ps_ref[:, sl]).astype(BF16)

    n_taps = dww_ref.shape[0]
    pad = n_taps // 2
    z = dwb_ref[...]
    for k in range(n_taps):
        z = z + dww_ref[k:k + 1, :] * pzext_ref[pl.ds((hv - pad + k) * SUBLANES, tm), pool_w:pz_w]
    mu = jnp.mean(z, axis=-1, keepdims=True)
    zc = z - mu
    var = jnp.mean(zc * zc, axis=-1, keepdims=True)
    zn = zc * lax.rsqrt(var + LN_EPS) * lng_ref[...] + lnb_ref[...]
    yp_ref[:, pool_w:pz_w] = (zn * jax.nn.sigmoid(zn)).astype(BF16)

    lru_taps = cw_ref.shape[0]
    lh = lru_taps - 1
    for j in range(lh):
        main_tail = xr_ref[rows(chunk - lh + j), :]
        before = jnp.where(first, 0.0, xrtail_ref[rows(j), :])
        _wrap_rows(xext_ref, j * SUBLANES, main_tail, before, forward=False)
        xrtail_ref[rows(j), :] = main_tail
    xext_ref[lh * SUBLANES:, :] = xr_ref[...]
    offsets = tuple(k * SUBLANES for k in range(lru_taps))
    _lru_coeffs(xext_ref, offsets, cw_ref, cb_ref, wax_ref, bax_ref, lam_ref, a_ref, b_ref)
    _lru_scan(a_ref, b_ref, hf_ref, carry_ref, reverse=False)
    yp_ref[:, pz_w:] = ((hf_ref[...] + hb_ref[...]) * ua_ref[:, pz_w:]).astype(BF16)

    y_ref[...] = jnp.dot(unperm_ref[...], yp_ref[...], preferred_element_type=F32).astype(BF16)


def _mix_body(*refs, seq_len, n_tiles):
    (x_ref, u_ref, unext_ref,
     pw_ref, ps_ref, dww_ref, dwb_ref, lng_ref, lnb_ref,
     cw_ref, cb_ref, wax_ref, bax_ref, lam_ref, wout_ref, unperm_ref, o_ref,
     pzext_ref, pztail_ref, xext_ref, xrtail_ref, a_ref, b_ref, hf_ref, yp_ref, y0_ref, y1_ref,
     carry_ref) = refs
    step = pl.program_id(0)

    @pl.when(step == 0)
    def _():
        carry_ref[...] = jnp.zeros_like(carry_ref)
        y1_ref[...] = jnp.zeros_like(y1_ref)
        pztail_ref[...] = jnp.zeros_like(pztail_ref)
        xrtail_ref[...] = jnp.zeros_like(xrtail_ref)

    run = functools.partial(
        _mix_step, x_ref=x_ref, u_ref=u_ref, unext_ref=unext_ref, pw_ref=pw_ref, ps_ref=ps_ref,
        dww_ref=dww_ref, dwb_ref=dwb_ref, lng_ref=lng_ref, lnb_ref=lnb_ref, cw_ref=cw_ref, cb_ref=cb_ref,
        wax_ref=wax_ref, bax_ref=bax_ref, lam_ref=lam_ref, wout_ref=wout_ref, unperm_ref=unperm_ref,
        o_ref=o_ref, pzext_ref=pzext_ref, pztail_ref=pztail_ref, xext_ref=xext_ref, xrtail_ref=xrtail_ref,
        a_ref=a_ref, b_ref=b_ref, hf_ref=hf_ref, yp_ref=yp_ref, carry_ref=carry_ref,
        seq_len=seq_len, n_tiles=n_tiles)

    @pl.when(step % 2 == 0)
    def _():
        run(y1_ref, y0_ref)

    @pl.when(step % 2 == 1)
    def _():
        run(y0_ref, y1_ref)


def _mix(x, u, unperm, pool_w, pool_scale, dw_w, dw_b, ln_g, ln_b, cw, cb, wax, bax, lam,
         w_out_bf16, layer, *, tm):
    s, d = x.shape
    nt = s // tm
    n_pool_groups, pg = pool_w.shape[1], pool_w.shape[2]
    pool_width = n_pool_groups * pg
    conv_width = dw_w.shape[2]
    conv_taps = dw_w.shape[1]
    lru_taps = cw.shape[2]
    n_heads, hd = wax.shape[2], wax.shape[3]
    c = n_heads * hd
    d_mix = w_out_bf16.shape[1]
    n_u = u.shape[1]
    pz_w = pool_width + conv_width
    halo_rows = HALO_GROUPS * SUBLANES
    assert conv_taps // 2 <= HALO_GROUPS and max(POOL_WINDOWS) // 2 <= HALO_GROUPS
    assert pz_w + c == d_mix == n_u - 2 * c
    assert tm % halo_rows == 0 and pz_w % 128 == 0 and n_u % pz_w == 0
    halo_per_tile = tm // halo_rows
    last_halo = s // halo_rows - 1

    tile = lambda st: jnp.minimum(st, nt - 1)
    cur = lambda st: (tile(st), 0)
    prev = lambda st: (jnp.maximum(st - 1, 0), 0)
    body = functools.partial(_mix_body, seq_len=s, n_tiles=nt)
    return pl.pallas_call(
        body,
        grid=(nt + 1,),
        in_specs=[
            pl.BlockSpec((tm, d), prev),
            pl.BlockSpec((tm, n_u), cur),
            pl.BlockSpec((halo_rows, pz_w),
                         lambda st: (jnp.minimum((tile(st) + 1) * halo_per_tile, last_halo), 0)),
            _layer_block((n_pool_groups, pg, pg), layer),
            _layer_block((1, pool_width), layer),
            _layer_block((conv_taps, conv_width), layer),
            _layer_block((1, conv_width), layer),
            _layer_block((1, conv_width), layer),
            _layer_block((1, conv_width), layer),
            _dir_block((lru_taps, c), layer, 0),
            _dir_block((1, c), layer, 0),
            _dir_block((n_heads, hd, 2 * hd), layer, 0),
            _dir_block((n_heads, 1, 2 * hd), layer, 0),
            _dir_block((1, c), layer, 0),
            _layer_block((d_mix, d), layer),
            pl.BlockSpec((tm, tm), lambda st: (0, 0), pipeline_mode=pl.Buffered(1)),
        ],
        out_specs=pl.BlockSpec((tm, d), prev),
        out_shape=jax.ShapeDtypeStruct((s, d), F32),
        scratch_shapes=[
            pltpu.VMEM((tm + 2 * halo_rows, pz_w), F32),
            pltpu.VMEM((halo_rows, pz_w), F32),
            pltpu.VMEM((tm + (lru_taps - 1) * SUBLANES, c), F32),
            pltpu.VMEM(((lru_taps - 1) * SUBLANES, c), F32),
            pltpu.VMEM((tm, c), F32),
            pltpu.VMEM((tm, c), F32),
            pltpu.VMEM((tm, c), F32),
            pltpu.VMEM((tm, d_mix), BF16),
            pltpu.VMEM((tm, d_mix), BF16),
            pltpu.VMEM((tm, d_mix), BF16),
            pltpu.VMEM((SUBLANES, c), F32),
        ],
        compiler_params=pltpu.CompilerParams(
            dimension_semantics=("arbitrary",), vmem_limit_bytes=V7X_VMEM_LIMIT_BYTES),
        name="mix",
    )(x, u, u, pool_w, pool_scale, dw_w, dw_b, ln_g, ln_b, cw, cb, wax, bax, lam, w_out_bf16, unperm)


def _chunk_permutation(tm):
    chunk = tm // SUBLANES
    pos = jnp.arange(tm)
    src = (pos % SUBLANES) * chunk + pos // SUBLANES
    return (src[:, None] == jnp.arange(tm)[None, :]).astype(BF16)


def kernel(x, norm_ffn1, ffn1_w_gate, ffn1_w_up, ffn1_w_down, norm_mix, w_in, pool_w, pool_scale, conv_dw_w, conv_dw_b, conv_ln_g, conv_ln_b, lru_conv_w, lru_conv_b, lru_w_a, lru_b_a, lru_w_x, lru_b_x, lru_lambda, w_out, norm_ffn2, ffn2_w_gate, ffn2_w_up, ffn2_w_down, norm_final):
    b, s, d = x.shape
    depth = w_in.shape[0]
    n_dirs, n_heads, hd = lru_w_a.shape[1], lru_w_a.shape[2], lru_w_a.shape[3]

    row = lambda p: p[:, None, :]
    norm_ffn1_r, norm_mix_r, norm_ffn2_r = row(norm_ffn1), row(norm_mix), row(norm_ffn2)
    norm_final_r = norm_final[None, :]
    pool_scale_r, dw_b_r, ln_g_r, ln_b_r = row(pool_scale), row(conv_dw_b), row(conv_ln_g), row(conv_ln_b)
    lru_cb_r = lru_conv_b[:, :, None, :]
    lam_r = lru_lambda[:, :, None, :]
    wax = jnp.concatenate([lru_w_a, lru_w_x], axis=-1).astype(BF16)
    bax = jnp.concatenate([lru_b_a.reshape(depth, n_dirs, n_heads, 1, hd),
                           lru_b_x.reshape(depth, n_dirs, n_heads, 1, hd)], axis=-1)
    w_in_b = w_in.astype(BF16)
    w_out_b = w_out.astype(BF16)
    pool_w_b = pool_w.astype(BF16)
    tm_mix = min(MIX_ROWS, s)
    perm = _chunk_permutation(tm_mix)
    unperm = perm.T
    pool_width = pool_w.shape[1] * pool_w.shape[2]
    conv_width = conv_dw_w.shape[2]

    outs = []
    for bi in range(b):
        xb = x[bi]
        for l in range(depth):
            xb = _ffn(xb, norm_ffn1_r, ffn1_w_gate, ffn1_w_up, ffn1_w_down, norm_final_r, l,
                      final_norm=False, tm=FFN_ROWS, tf=FFN_COLS)
            u = _proj_bwd(xb, norm_mix_r, perm, w_in_b, lru_conv_w, lru_cb_r, wax, bax, lam_r, l,
                          tm=tm_mix, pool_w=pool_width, conv_w=conv_width)
            xb = _mix(xb, u, unperm, pool_w_b, pool_scale_r, conv_dw_w, dw_b_r, ln_g_r, ln_b_r,
                      lru_conv_w, lru_cb_r, wax, bax, lam_r, w_out_b, l, tm=tm_mix)
            xb = _ffn(xb, norm_ffn2_r, ffn2_w_gate, ffn2_w_up, ffn2_w_down, norm_final_r, l,
                      final_norm=(l == depth - 1), tm=FFN_ROWS, tf=FFN_COLS)
        outs.append(xb)
    return jnp.stack(outs, axis=0) if b > 1 else outs[0][None]
```

```python
import functools

import jax
import jax.numpy as jnp
from jax import lax
from jax.experimental import pallas as pl
from jax.experimental.pallas import tpu as pltpu

RMS_EPS = 1e-6
LN_EPS = 1e-5
LRU_C = 8.0
POOL_WINDOWS = (2, 4, 8, 16)

SUBLANES = 8
HALO_GROUPS = 16
V7X_VMEM_LIMIT_BYTES = 56 * 1024 * 1024

F32 = jnp.float32
BF16 = jnp.bfloat16

FFN_ROWS = 1024
FFN_COLS = 512
MIX_ROWS = 256
PROJ_SUBTILES = 1


def _rms(x, g):
    return x * lax.rsqrt(jnp.mean(x * x, axis=-1, keepdims=True) + RMS_EPS) * g


def _layer_block(shape, layer):
    zeros = (0,) * len(shape)
    return pl.BlockSpec((None,) + tuple(shape), lambda *_: (layer,) + zeros, pipeline_mode=pl.Buffered(1))


def _dir_block(shape, layer, direction):
    zeros = (0,) * len(shape)
    return pl.BlockSpec((None, None) + tuple(shape), lambda *_: (layer, direction) + zeros,
                        pipeline_mode=pl.Buffered(1))


def _ffn_body(x_hbm, g_ref, wg_ref, wu_ref, wd_ref, gf_ref, o_hbm, acc_ref, h_ref, in_sem, out_sem,
              *, n_ff, d_chunk, final_norm):
    i = pl.program_id(0)
    j = pl.program_id(1)
    n_tiles = pl.num_programs(0)
    tm = acc_ref.shape[1]
    slot = i % 2
    other = 1 - slot

    def x_copy(tile, s):
        return pltpu.make_async_copy(x_hbm.at[pl.ds(tile * tm, tm)], acc_ref.at[s], in_sem.at[s])

    def o_copy(tile, s):
        return pltpu.make_async_copy(acc_ref.at[s], o_hbm.at[pl.ds(tile * tm, tm)], out_sem.at[s])

    @pl.when(j == 0)
    def _():
        @pl.when(i == 0)
        def _():
            x_copy(0, 0).start()

        x_copy(i, slot).wait()
        h_ref[...] = _rms(acc_ref[slot], g_ref[...]).astype(BF16)

    @pl.when(j == min(2, n_ff - 1))
    def _():
        @pl.when(i >= 1)
        def _():
            o_copy(i - 1, other).wait()

        @pl.when(i + 1 < n_tiles)
        def _():
            x_copy(i + 1, other).start()

    h = h_ref[...]
    gate = jnp.dot(h, wg_ref[...].astype(BF16), preferred_element_type=F32)
    up = jnp.dot(h, wu_ref[...].astype(BF16), preferred_element_type=F32)
    act = ((gate * jax.nn.sigmoid(gate)) * (0.5 * up)).astype(BF16)
    d = acc_ref.shape[2]
    for c in range(0, d, d_chunk):
        acc_ref[slot, :, c:c + d_chunk] += jnp.dot(act, wd_ref[:, c:c + d_chunk].astype(BF16),
                                                   preferred_element_type=F32)

    @pl.when(j == n_ff - 1)
    def _():
        if final_norm:
            acc_ref[slot] = _rms(acc_ref[slot], gf_ref[...])
        o_copy(i, slot).start()

        @pl.when(i == n_tiles - 1)
        def _():
            o_copy(i, slot).wait()


def _ffn(x, norm_g, w_gate, w_up, w_down, norm_final, layer, *, final_norm, tm, tf):
    s, d = x.shape
    f = w_gate.shape[-1]
    tm = min(tm, s)
    tf = min(tf, f)
    n_ff = f // tf
    assert n_ff >= 2
    body = functools.partial(_ffn_body, n_ff=n_ff, d_chunk=min(512, d), final_norm=final_norm)
    return pl.pallas_call(
        body,
        grid=(s // tm, n_ff),
        in_specs=[
            pl.BlockSpec(memory_space=pl.ANY),
            pl.BlockSpec((None, 1, d), lambda i, j: (layer, 0, 0)),
            pl.BlockSpec((None, d, tf), lambda i, j: (layer, 0, j)),
            pl.BlockSpec((None, d, tf), lambda i, j: (layer, 0, j)),
            pl.BlockSpec((None, tf, d), lambda i, j: (layer, j, 0)),
            pl.BlockSpec((1, d), lambda i, j: (0, 0)),
        ],
        out_specs=pl.BlockSpec(memory_space=pl.ANY),
        out_shape=jax.ShapeDtypeStruct((s, d), F32),
        scratch_shapes=[pltpu.VMEM((2, tm, d), F32), pltpu.VMEM((tm, d), BF16),
                        pltpu.SemaphoreType.DMA((2,)), pltpu.SemaphoreType.DMA((2,))],
        compiler_params=pltpu.CompilerParams(
            dimension_semantics=("arbitrary", "arbitrary"),
            vmem_limit_bytes=V7X_VMEM_LIMIT_BYTES),
        name="ffn",
    )(x, norm_g, w_gate, w_up, w_down, norm_final)


def _lru_coeffs(ext_ref, offsets, cw_ref, cb_ref, wax_ref, bax_ref, lam_ref, a_ref, b_ref):
    tm, c = a_ref.shape
    n_heads = wax_ref.shape[0]
    hd = c // n_heads
    xc = cb_ref[...] + cw_ref[0:1, :] * ext_ref[pl.ds(offsets[0], tm), :]
    for k in range(1, len(offsets)):
        xc = xc + cw_ref[k:k + 1, :] * ext_ref[pl.ds(offsets[k], tm), :]
    softplus_neg_lam = jax.nn.softplus(-lam_ref[...])
    for h in range(n_heads):
        sl = slice(h * hd, (h + 1) * hd)
        xh = xc[:, sl]
        ri = jnp.dot(xh.astype(BF16), wax_ref[h], preferred_element_type=F32) + bax_ref[h]
        r = jax.nn.sigmoid(ri[:, :hd])
        ig = jax.nn.sigmoid(ri[:, hd:])
        log_a = -LRU_C * r * softplus_neg_lam[:, sl]
        a = jnp.exp(log_a)
        a_ref[:, sl] = a
        v = 1.0 - a * a
        b_ref[:, sl] = jnp.where(v > 0.0, v * lax.rsqrt(v), 0.0) * (ig * xh)


def _sublane_scan(a, b, *, reverse):
    row = lax.broadcasted_iota(jnp.int32, a.shape, 0)
    for dist in (1, 2, 4):
        if reverse:
            keep = row < SUBLANES - dist
            shift = SUBLANES - dist
        else:
            keep = row >= dist
            shift = dist
        a_n = jnp.where(keep, pltpu.roll(a, shift, 0), 1.0)
        b_n = jnp.where(keep, pltpu.roll(b, shift, 0), 0.0)
        b = a * b_n + b
        a = a * a_n
    return a, b


def _lru_scan(a_ref, b_ref, h_ref, carry_ref, *, reverse):
    tm, c = a_ref.shape
    n_groups = tm // SUBLANES
    order = range(n_groups - 1, -1, -1) if reverse else range(n_groups)
    rows = lambda j: slice(j * SUBLANES, (j + 1) * SUBLANES)
    row = lax.broadcasted_iota(jnp.int32, (SUBLANES, c), 0)

    h = jnp.zeros((SUBLANES, c), F32)
    p = jnp.ones((SUBLANES, c), F32)
    for j in order:
        a = a_ref[rows(j), :]
        h = a * h + b_ref[rows(j), :]
        p = a * p
    a_cum, b_cum = _sublane_scan(p, h, reverse=reverse)
    carry = carry_ref[...]
    h_edge = a_cum * carry + b_cum
    if reverse:
        h = jnp.where(row == SUBLANES - 1, carry, pltpu.roll(h_edge, SUBLANES - 1, 0))
        carry_ref[...] = jnp.broadcast_to(h_edge[0:1, :], (SUBLANES, c))
    else:
        h = jnp.where(row == 0, carry, pltpu.roll(h_edge, 1, 0))
        carry_ref[...] = jnp.broadcast_to(h_edge[SUBLANES - 1:SUBLANES, :], (SUBLANES, c))
    for j in order:
        h = a_ref[rows(j), :] * h + b_ref[rows(j), :]
        h_ref[rows(j), :] = h


def _wrap_rows(dst_ref, dst_row, inner, outer, *, forward):
    row = lax.broadcasted_iota(jnp.int32, inner.shape, 0)
    if forward:
        val = jnp.where(row == SUBLANES - 1, pltpu.roll(outer, SUBLANES - 1, 0),
                        pltpu.roll(inner, SUBLANES - 1, 0))
    else:
        val = jnp.where(row == 0, pltpu.roll(outer, 1, 0), pltpu.roll(inner, 1, 0))
    dst_ref[dst_row:dst_row + SUBLANES, :] = val


def _proj_bwd_body(*refs, pool_w, conv_w, tm):
    n_sub = PROJ_SUBTILES
    (x_ref, g_ref, perm_ref, w_ref, cw_ref, cb_ref, wax_ref, bax_ref, lam_ref,
     u_ref, lhs_ref) = refs[:11]
    ext_refs = refs[11:11 + n_sub]
    a_refs = refs[11 + n_sub:11 + 2 * n_sub]
    b_refs = refs[11 + 2 * n_sub:11 + 3 * n_sub]
    carry_ref = refs[11 + 3 * n_sub]
    i = pl.program_id(0)
    c = cw_ref.shape[1]
    n_in = w_ref.shape[1] - c
    n_a = n_in - conv_w
    ua_ref = u_ref.at[:, 0:n_a]
    xr_ref = u_ref.at[:, n_a:n_a + c]
    n_taps = cw_ref.shape[0]
    halo = (n_taps - 1) * SUBLANES
    pz_w = pool_w + conv_w
    v0, lg0 = pool_w, pool_w + 2 * conv_w

    @pl.when(i == 0)
    def _():
        carry_ref[...] = jnp.zeros_like(carry_ref)
        ext_refs[0][0:halo, :] = jnp.zeros((halo, c), F32)

    h = _rms(x_ref[...], g_ref[...]).astype(BF16)
    for sub in range(n_sub):
        r = slice(sub * tm, (sub + 1) * tm)
        lhs_ref[r, :] = jnp.dot(perm_ref[...], h[r, :], preferred_element_type=F32).astype(BF16)
    xr = jnp.dot(lhs_ref[...], w_ref[:, n_in:], preferred_element_type=F32)
    xr_ref[...] = xr
    ua_ref[:, pz_w:] = jax.nn.gelu(jnp.dot(lhs_ref[...], w_ref[:, lg0:n_in], preferred_element_type=F32))
    vg = jnp.dot(lhs_ref[...], w_ref[:, v0:lg0], preferred_element_type=F32)
    ua_ref[:, pool_w:pz_w] = vg[:, :conv_w] * jax.nn.sigmoid(vg[:, conv_w:])
    ua_ref[:, 0:pool_w] = jnp.dot(lhs_ref[...], w_ref[:, 0:pool_w], preferred_element_type=F32)

    offsets = tuple((n_taps - 1 - k) * SUBLANES for k in range(n_taps))
    for sub in reversed(range(n_sub)):
        r0 = sub * tm
        ext_ref = ext_refs[sub]
        after_ref = ext_refs[(sub + 1) % n_sub]
        for j in range(n_taps - 1):
            rows = slice(j * SUBLANES, (j + 1) * SUBLANES)
            _wrap_rows(ext_ref, tm + j * SUBLANES, xr[r0 + j * SUBLANES:r0 + (j + 1) * SUBLANES, :],
                       after_ref[rows, :], forward=True)
        ext_ref[0:tm, :] = xr[r0:r0 + tm, :]
        _lru_coeffs(ext_ref, offsets, cw_ref, cb_ref, wax_ref, bax_ref, lam_ref, a_refs[sub], b_refs[sub])
        _lru_scan(a_refs[sub], b_refs[sub], u_ref.at[r0:r0 + tm, n_a + c:], carry_ref, reverse=True)


def _proj_bwd(x, norm_g, perm, w_in_bf16, cw, cb, wax, bax, lam, layer, *, tm, pool_w, conv_w):
    s, d = x.shape
    d_in = w_in_bf16.shape[-1]
    n_taps = cw.shape[2]
    n_heads, hd = wax.shape[2], wax.shape[3]
    c = n_heads * hd
    na = d_in - c - conv_w
    n_sub = PROJ_SUBTILES
    rows_blk = n_sub * tm
    assert s % rows_blk == 0
    nb = s // rows_blk
    halo = (n_taps - 1) * SUBLANES
    rev = lambda i: (nb - 1 - i, 0)
    body = functools.partial(_proj_bwd_body, pool_w=pool_w, conv_w=conv_w, tm=tm)
    return pl.pallas_call(
        body,
        grid=(nb,),
        in_specs=[
            pl.BlockSpec((rows_blk, d), rev),
            _layer_block((1, d), layer),
            pl.BlockSpec((tm, tm), lambda i: (0, 0), pipeline_mode=pl.Buffered(1)),
            _layer_block((d, d_in), layer),
            _dir_block((n_taps, c), layer, 1),
            _dir_block((1, c), layer, 1),
            _dir_block((n_heads, hd, 2 * hd), layer, 1),
            _dir_block((n_heads, 1, 2 * hd), layer, 1),
            _dir_block((1, c), layer, 1),
        ],
        out_specs=pl.BlockSpec((rows_blk, na + 2 * c), rev),
        out_shape=jax.ShapeDtypeStruct((s, na + 2 * c), F32),
        scratch_shapes=([pltpu.VMEM((rows_blk, d), BF16)]
                        + [pltpu.VMEM((tm + halo, c), F32)] * n_sub
                        + [pltpu.VMEM((tm, c), F32)] * (2 * n_sub)
                        + [pltpu.VMEM((SUBLANES, c), F32)]),
        compiler_params=pltpu.CompilerParams(
            dimension_semantics=("arbitrary",), vmem_limit_bytes=V7X_VMEM_LIMIT_BYTES),
        name="proj_bwd",
    )(x, norm_g, perm, w_in_bf16, cw, cb, wax, bax, lam)


def _mix_step(y_prev_ref, y_ref, x_ref, u_ref, unext_ref,
              pw_ref, ps_ref, dww_ref, dwb_ref, lng_ref, lnb_ref,
              cw_ref, cb_ref, wax_ref, bax_ref, lam_ref, wout_ref, unperm_ref, o_ref,
              pzext_ref, pztail_ref, xext_ref, xrtail_ref, a_ref, b_ref, hf_ref, yp_ref, carry_ref,
              *, seq_len, n_tiles):
    step = pl.program_id(0)
    i = jnp.minimum(step, n_tiles - 1)
    tm = u_ref.shape[0]
    chunk = tm // SUBLANES
    first = i == 0
    last = i == n_tiles - 1
    n_groups, pg = pw_ref.shape[0], pw_ref.shape[1]
    pool_w = n_groups * pg
    conv_w = dww_ref.shape[1]
    pz_w = pool_w + conv_w
    c = cw_ref.shape[1]
    n_a = u_ref.shape[1] - 2 * c
    ua_ref = u_ref.at[:, 0:n_a]
    xr_ref = u_ref.at[:, n_a:n_a + c]
    hb_ref = u_ref.at[:, n_a + c:]
    hv = HALO_GROUPS
    rows = lambda j: slice(j * SUBLANES, (j + 1) * SUBLANES)

    o_ref[...] = x_ref[...] + jnp.dot(y_prev_ref[...], wout_ref[...], preferred_element_type=F32)

    for j in range(hv):
        main_tail = ua_ref[rows(chunk - hv + j), 0:pz_w]
        before = jnp.where(first, 0.0, pztail_ref[rows(j), :])
        _wrap_rows(pzext_ref, j * SUBLANES, main_tail, before, forward=False)
        pztail_ref[rows(j), :] = main_tail
        after = jnp.where(last, 0.0, unext_ref[rows(j), :])
        _wrap_rows(pzext_ref, (hv + chunk + j) * SUBLANES, ua_ref[rows(j), 0:pz_w], after, forward=True)
    pzext_ref[hv * SUBLANES:hv * SUBLANES + tm, :] = ua_ref[:, 0:pz_w]

    pos = lax.broadcasted_iota(jnp.int32, (tm, pg), 0)
    t = i * tm + (pos & (SUBLANES - 1)) * chunk + (pos >> 3)
    for g, win in enumerate(POOL_WINDOWS):
        sl = slice(g * pg, (g + 1) * pg)
        half = win // 2
        ssum = pzext_ref[pl.ds((hv - half) * SUBLANES, tm), sl]
        for o in range(-half + 1, half):
            ssum = ssum + pzext_ref[pl.ds((hv + o) * SUBLANES, tm), sl]
        cnt = (jnp.minimum(t + half, seq_len) - jnp.maximum(t - half, 0)).astype(F32)
        pooled = ssum / cnt - ua_ref[:, sl]
        mixed = jnp.dot(pooled.astype(BF16), pw_ref[g], preferred_element_type=F32)
        yp_ref[:, sl] = (mixed * ps_ref[:, sl]).astype(BF16)

    n_taps = dww_ref.shape[0]
    pad = n_taps // 2
    z = dwb_ref[...]
    for k in range(n_taps):
        z = z + dww_ref[k:k + 1, :] * pzext_ref[pl.ds((hv - pad + k) * SUBLANES, tm), pool_w:pz_w]
    mu = jnp.mean(z, axis=-1, keepdims=True)
    zc = z - mu
    var = jnp.mean(zc * zc, axis=-1, keepdims=True)
    zn = zc * lax.rsqrt(var + LN_EPS) * lng_ref[...] + lnb_ref[...]
    yp_ref[:, pool_w:pz_w] = (zn * jax.nn.sigmoid(zn)).astype(BF16)

    lru_taps = cw_ref.shape[0]
    lh = lru_taps - 1
    for j in range(lh):
        main_tail = xr_ref[rows(chunk - lh + j), :]
        before = jnp.where(first, 0.0, xrtail_ref[rows(j), :])
        _wrap_rows(xext_ref, j * SUBLANES, main_tail, before, forward=False)
        xrtail_ref[rows(j), :] = main_tail
    xext_ref[lh * SUBLANES:, :] = xr_ref[...]
    offsets = tuple(k * SUBLANES for k in range(lru_taps))
    _lru_coeffs(xext_ref, offsets, cw_ref, cb_ref, wax_ref, bax_ref, lam_ref, a_ref, b_ref)
    _lru_scan(a_ref, b_ref, hf_ref, carry_ref, reverse=False)
    yp_ref[:, pz_w:] = ((hf_ref[...] + hb_ref[...]) * ua_ref[:, pz_w:]).astype(BF16)

    y_ref[...] = jnp.dot(unperm_ref[...], yp_ref[...], preferred_element_type=F32).astype(BF16)


def _mix_body(*refs, seq_len, n_tiles):
    (x_ref, u_ref, unext_ref,
     pw_ref, ps_ref, dww_ref, dwb_ref, lng_ref, lnb_ref,
     cw_ref, cb_ref, wax_ref, bax_ref, lam_ref, wout_ref, unperm_ref, o_ref,
     pzext_ref, pztail_ref, xext_ref, xrtail_ref, a_ref, b_ref, hf_ref, yp_ref, y_ref,
     carry_ref) = refs
    step = pl.program_id(0)

    @pl.when(step == 0)
    def _():
        carry_ref[...] = jnp.zeros_like(carry_ref)
        y_ref[1] = jnp.zeros(y_ref.shape[1:], y_ref.dtype)
        pztail_ref[...] = jnp.zeros_like(pztail_ref)
        xrtail_ref[...] = jnp.zeros_like(xrtail_ref)

    run = functools.partial(
        _mix_step, x_ref=x_ref, u_ref=u_ref, unext_ref=unext_ref, pw_ref=pw_ref, ps_ref=ps_ref,
        dww_ref=dww_ref, dwb_ref=dwb_ref, lng_ref=lng_ref, lnb_ref=lnb_ref, cw_ref=cw_ref, cb_ref=cb_ref,
        wax_ref=wax_ref, bax_ref=bax_ref, lam_ref=lam_ref, wout_ref=wout_ref, unperm_ref=unperm_ref,
        o_ref=o_ref, pzext_ref=pzext_ref, pztail_ref=pztail_ref, xext_ref=xext_ref, xrtail_ref=xrtail_ref,
        a_ref=a_ref, b_ref=b_ref, hf_ref=hf_ref, yp_ref=yp_ref, carry_ref=carry_ref,
        seq_len=seq_len, n_tiles=n_tiles)

    slot = step % 2
    run(y_ref.at[1 - slot], y_ref.at[slot])


def _mix(x, u, unperm, pool_w, pool_scale, dw_w, dw_b, ln_g, ln_b, cw, cb, wax, bax, lam,
         w_out_bf16, layer, *, tm):
    s, d = x.shape
    nt = s // tm
    n_pool_groups, pg = pool_w.shape[1], pool_w.shape[2]
    pool_width = n_pool_groups * pg
    conv_width = dw_w.shape[2]
    conv_taps = dw_w.shape[1]
    lru_taps = cw.shape[2]
    n_heads, hd = wax.shape[2], wax.shape[3]
    c = n_heads * hd
    d_mix = w_out_bf16.shape[1]
    n_u = u.shape[1]
    pz_w = pool_width + conv_width
    halo_rows = HALO_GROUPS * SUBLANES
    assert conv_taps // 2 <= HALO_GROUPS and max(POOL_WINDOWS) // 2 <= HALO_GROUPS
    assert pz_w + c == d_mix == n_u - 2 * c
    assert tm % halo_rows == 0 and pz_w % 128 == 0 and n_u % pz_w == 0
    halo_per_tile = tm // halo_rows
    last_halo = s // halo_rows - 1

    tile = lambda st: jnp.minimum(st, nt - 1)
    cur = lambda st: (tile(st), 0)
    prev = lambda st: (jnp.maximum(st - 1, 0), 0)
    body = functools.partial(_mix_body, seq_len=s, n_tiles=nt)
    return pl.pallas_call(
        body,
        grid=(nt + 1,),
        in_specs=[
            pl.BlockSpec((tm, d), prev),
            pl.BlockSpec((tm, n_u), cur),
            pl.BlockSpec((halo_rows, pz_w),
                         lambda st: (jnp.minimum((tile(st) + 1) * halo_per_tile, last_halo), 0)),
            _layer_block((n_pool_groups, pg, pg), layer),
            _layer_block((1, pool_width), layer),
            _layer_block((conv_taps, conv_width), layer),
            _layer_block((1, conv_width), layer),
            _layer_block((1, conv_width), layer),
            _layer_block((1, conv_width), layer),
            _dir_block((lru_taps, c), layer, 0),
            _dir_block((1, c), layer, 0),
            _dir_block((n_heads, hd, 2 * hd), layer, 0),
            _dir_block((n_heads, 1, 2 * hd), layer, 0),
            _dir_block((1, c), layer, 0),
            _layer_block((d_mix, d), layer),
            pl.BlockSpec((tm, tm), lambda st: (0, 0), pipeline_mode=pl.Buffered(1)),
        ],
        out_specs=pl.BlockSpec((tm, d), prev),
        out_shape=jax.ShapeDtypeStruct((s, d), F32),
        scratch_shapes=[
            pltpu.VMEM((tm + 2 * halo_rows, pz_w), F32),
            pltpu.VMEM((halo_rows, pz_w), F32),
            pltpu.VMEM((tm + (lru_taps - 1) * SUBLANES, c), F32),
            pltpu.VMEM(((lru_taps - 1) * SUBLANES, c), F32),
            pltpu.VMEM((tm, c), F32),
            pltpu.VMEM((tm, c), F32),
            pltpu.VMEM((tm, c), F32),
            pltpu.VMEM((tm, d_mix), BF16),
            pltpu.VMEM((2, tm, d_mix), BF16),
            pltpu.VMEM((SUBLANES, c), F32),
        ],
        compiler_params=pltpu.CompilerParams(
            dimension_semantics=("arbitrary",), vmem_limit_bytes=V7X_VMEM_LIMIT_BYTES),
        name="mix",
    )(x, u, u, pool_w, pool_scale, dw_w, dw_b, ln_g, ln_b, cw, cb, wax, bax, lam, w_out_bf16, unperm)


def _chunk_permutation(tm):
    chunk = tm // SUBLANES
    pos = jnp.arange(tm)
    src = (pos % SUBLANES) * chunk + pos // SUBLANES
    return (src[:, None] == jnp.arange(tm)[None, :]).astype(BF16)


def kernel(x, norm_ffn1, ffn1_w_gate, ffn1_w_up, ffn1_w_down, norm_mix, w_in, pool_w, pool_scale, conv_dw_w, conv_dw_b, conv_ln_g, conv_ln_b, lru_conv_w, lru_conv_b, lru_w_a, lru_b_a, lru_w_x, lru_b_x, lru_lambda, w_out, norm_ffn2, ffn2_w_gate, ffn2_w_up, ffn2_w_down, norm_final):
    b, s, d = x.shape
    depth = w_in.shape[0]
    n_dirs, n_heads, hd = lru_w_a.shape[1], lru_w_a.shape[2], lru_w_a.shape[3]

    row = lambda p: p[:, None, :]
    norm_ffn1_r, norm_mix_r, norm_ffn2_r = row(norm_ffn1), row(norm_mix), row(norm_ffn2)
    norm_final_r = norm_final[None, :]
    pool_scale_r, dw_b_r, ln_g_r, ln_b_r = row(pool_scale), row(conv_dw_b), row(conv_ln_g), row(conv_ln_b)
    lru_cb_r = lru_conv_b[:, :, None, :]
    lam_r = lru_lambda[:, :, None, :]
    wax = jnp.concatenate([lru_w_a, lru_w_x], axis=-1).astype(BF16)
    bax = jnp.concatenate([lru_b_a.reshape(depth, n_dirs, n_heads, 1, hd),
                           lru_b_x.reshape(depth, n_dirs, n_heads, 1, hd)], axis=-1)
    w_in_b = w_in.astype(BF16)
    w_out_b = w_out.astype(BF16)
    pool_w_b = pool_w.astype(BF16)
    tm_mix = min(MIX_ROWS, s)
    perm = _chunk_permutation(tm_mix)
    unperm = perm.T
    pool_width = pool_w.shape[1] * pool_w.shape[2]
    conv_width = conv_dw_w.shape[2]

    outs = []
    for bi in range(b):
        xb = x[bi]
        for l in range(depth):
            xb = _ffn(xb, norm_ffn1_r, ffn1_w_gate, ffn1_w_up, ffn1_w_down, norm_final_r, l,
                      final_norm=False, tm=FFN_ROWS, tf=FFN_COLS)
            u = _proj_bwd(xb, norm_mix_r, perm, w_in_b, lru_conv_w, lru_cb_r, wax, bax, lam_r, l,
                          tm=tm_mix, pool_w=pool_width, conv_w=conv_width)
            xb = _mix(xb, u, unperm, pool_w_b, pool_scale_r, conv_dw_w, dw_b_r, ln_g_r, ln_b_r,
                      lru_conv_w, lru_cb_r, wax, bax, lam_r, w_out_b, l, tm=tm_mix)
            xb = _ffn(xb, norm_ffn2_r, ffn2_w_gate, ffn2_w_up, ffn2_w_down, norm_final_r, l,
                      final_norm=(l == depth - 1), tm=FFN_ROWS, tf=FFN_COLS)
        outs.append(xb)
    return jnp.stack(outs, axis=0) if b > 1 else outs[0][None]
```

```python
import functools

import jax
import jax.numpy as jnp
from jax import lax
from jax.experimental import pallas as pl
from jax.experimental.pallas import tpu as pltpu

RMS_EPS = 1e-6
LN_EPS = 1e-5
LRU_C = 8.0
POOL_WINDOWS = (2, 4, 8, 16)

SUBLANES = 8
HALO_GROUPS = 16
V7X_VMEM_LIMIT_BYTES = 56 * 1024 * 1024

F32 = jnp.float32
BF16 = jnp.bfloat16

FFN_ROWS = 1024
FFN_COLS = 512
FFN_OUT_COLS = 512
MIX_ROWS = 256
PROJ_SUBTILES = 1


def _rms(x, g):
    return x * lax.rsqrt(jnp.mean(x * x, axis=-1, keepdims=True) + RMS_EPS) * g


def _layer_block(shape, layer):
    zeros = (0,) * len(shape)
    return pl.BlockSpec((None,) + tuple(shape), lambda *_: (layer,) + zeros, pipeline_mode=pl.Buffered(1))


def _dir_block(shape, layer, direction):
    zeros = (0,) * len(shape)
    return pl.BlockSpec((None, None) + tuple(shape), lambda *_: (layer, direction) + zeros,
                        pipeline_mode=pl.Buffered(1))


def _ffn_body(x_hbm, g_ref, wg_ref, wu_ref, wd_ref, gf_ref, o_hbm, acc_ref, h_ref, in_sem, out_sem,
              *, n_ff, d_chunk, final_norm):
    i = pl.program_id(0)
    j = pl.program_id(1)
    n_tiles = pl.num_programs(0)
    tm = acc_ref.shape[1]
    slot = i % 2
    other = 1 - slot

    def x_copy(tile, s):
        return pltpu.make_async_copy(x_hbm.at[pl.ds(tile * tm, tm)], acc_ref.at[s], in_sem.at[s])

    def o_copy(tile, s):
        return pltpu.make_async_copy(acc_ref.at[s], o_hbm.at[pl.ds(tile * tm, tm)], out_sem.at[s])

    @pl.when(j == 0)
    def _():
        @pl.when(i == 0)
        def _():
            x_copy(0, 0).start()

        x_copy(i, slot).wait()
        h_ref[...] = _rms(acc_ref[slot], g_ref[...]).astype(BF16)

    @pl.when(j == min(2, n_ff - 1))
    def _():
        @pl.when(i >= 1)
        def _():
            o_copy(i - 1, other).wait()

        @pl.when(i + 1 < n_tiles)
        def _():
            x_copy(i + 1, other).start()

    h = h_ref[...]
    gate = jnp.dot(h, wg_ref[...].astype(BF16), preferred_element_type=F32)
    up = jnp.dot(h, wu_ref[...].astype(BF16), preferred_element_type=F32)
    act = ((gate * jax.nn.sigmoid(gate)) * (0.5 * up)).astype(BF16)
    d = acc_ref.shape[2]
    for c in range(0, d, d_chunk):
        acc_ref[slot, :, c:c + d_chunk] += jnp.dot(act, wd_ref[:, c:c + d_chunk].astype(BF16),
                                                   preferred_element_type=F32)

    @pl.when(j == n_ff - 1)
    def _():
        if final_norm:
            acc_ref[slot] = _rms(acc_ref[slot], gf_ref[...])
        o_copy(i, slot).start()

        @pl.when(i == n_tiles - 1)
        def _():
            o_copy(i, slot).wait()


def _ffn(x, norm_g, w_gate, w_up, w_down, norm_final, layer, *, final_norm, tm, tf):
    s, d = x.shape
    f = w_gate.shape[-1]
    tm = min(tm, s)
    tf = min(tf, f)
    n_ff = f // tf
    assert n_ff >= 2
    body = functools.partial(_ffn_body, n_ff=n_ff, d_chunk=min(FFN_OUT_COLS, d), final_norm=final_norm)
    return pl.pallas_call(
        body,
        grid=(s // tm, n_ff),
        in_specs=[
            pl.BlockSpec(memory_space=pl.ANY),
            pl.BlockSpec((None, 1, d), lambda i, j: (layer, 0, 0)),
            pl.BlockSpec((None, d, tf), lambda i, j: (layer, 0, j)),
            pl.BlockSpec((None, d, tf), lambda i, j: (layer, 0, j)),
            pl.BlockSpec((None, tf, d), lambda i, j: (layer, j, 0)),
            pl.BlockSpec((1, d), lambda i, j: (0, 0)),
        ],
        out_specs=pl.BlockSpec(memory_space=pl.ANY),
        out_shape=jax.ShapeDtypeStruct((s, d), F32),
        scratch_shapes=[pltpu.VMEM((2, tm, d), F32), pltpu.VMEM((tm, d), BF16),
                        pltpu.SemaphoreType.DMA((2,)), pltpu.SemaphoreType.DMA((2,))],
        compiler_params=pltpu.CompilerParams(
            dimension_semantics=("arbitrary", "arbitrary"),
            vmem_limit_bytes=V7X_VMEM_LIMIT_BYTES),
        name="ffn",
    )(x, norm_g, w_gate, w_up, w_down, norm_final)


def _lru_coeffs(ext_ref, offsets, cw_ref, cb_ref, wax_ref, bax_ref, lam_ref, a_ref, b_ref):
    tm, c = a_ref.shape
    n_heads = wax_ref.shape[0]
    hd = c // n_heads
    xc = cb_ref[...] + cw_ref[0:1, :] * ext_ref[pl.ds(offsets[0], tm), :]
    for k in range(1, len(offsets)):
        xc = xc + cw_ref[k:k + 1, :] * ext_ref[pl.ds(offsets[k], tm), :]
    softplus_neg_lam = jax.nn.softplus(-lam_ref[...])
    for h in range(n_heads):
        sl = slice(h * hd, (h + 1) * hd)
        xh = xc[:, sl]
        ri = jnp.dot(xh.astype(BF16), wax_ref[h], preferred_element_type=F32) + bax_ref[h]
        r = jax.nn.sigmoid(ri[:, :hd])
        ig = jax.nn.sigmoid(ri[:, hd:])
        log_a = -LRU_C * r * softplus_neg_lam[:, sl]
        a = jnp.exp(log_a)
        a_ref[:, sl] = a
        v = 1.0 - a * a
        b_ref[:, sl] = jnp.where(v > 0.0, v * lax.rsqrt(v), 0.0) * (ig * xh)


def _sublane_scan(a, b, *, reverse):
    row = lax.broadcasted_iota(jnp.int32, a.shape, 0)
    for dist in (1, 2, 4):
        if reverse:
            keep = row < SUBLANES - dist
            shift = SUBLANES - dist
        else:
            keep = row >= dist
            shift = dist
        a_n = jnp.where(keep, pltpu.roll(a, shift, 0), 1.0)
        b_n = jnp.where(keep, pltpu.roll(b, shift, 0), 0.0)
        b = a * b_n + b
        a = a * a_n
    return a, b


def _lru_scan(a_ref, b_ref, h_ref, carry_ref, *, reverse):
    tm, c = a_ref.shape
    n_groups = tm // SUBLANES
    order = range(n_groups - 1, -1, -1) if reverse else range(n_groups)
    rows = lambda j: slice(j * SUBLANES, (j + 1) * SUBLANES)
    row = lax.broadcasted_iota(jnp.int32, (SUBLANES, c), 0)

    h = jnp.zeros((SUBLANES, c), F32)
    p = jnp.ones((SUBLANES, c), F32)
    for j in order:
        a = a_ref[rows(j), :]
        h = a * h + b_ref[rows(j), :]
        p = a * p
    a_cum, b_cum = _sublane_scan(p, h, reverse=reverse)
    carry = carry_ref[...]
    h_edge = a_cum * carry + b_cum
    if reverse:
        h = jnp.where(row == SUBLANES - 1, carry, pltpu.roll(h_edge, SUBLANES - 1, 0))
        carry_ref[...] = jnp.broadcast_to(h_edge[0:1, :], (SUBLANES, c))
    else:
        h = jnp.where(row == 0, carry, pltpu.roll(h_edge, 1, 0))
        carry_ref[...] = jnp.broadcast_to(h_edge[SUBLANES - 1:SUBLANES, :], (SUBLANES, c))
    for j in order:
        h = a_ref[rows(j), :] * h + b_ref[rows(j), :]
        h_ref[rows(j), :] = h


def _wrap_rows(dst_ref, dst_row, inner, outer, *, forward):
    row = lax.broadcasted_iota(jnp.int32, inner.shape, 0)
    if forward:
        val = jnp.where(row == SUBLANES - 1, pltpu.roll(outer, SUBLANES - 1, 0),
                        pltpu.roll(inner, SUBLANES - 1, 0))
    else:
        val = jnp.where(row == 0, pltpu.roll(outer, 1, 0), pltpu.roll(inner, 1, 0))
    dst_ref[dst_row:dst_row + SUBLANES, :] = val


def _proj_bwd_body(*refs, pool_w, conv_w, tm):
    n_sub = PROJ_SUBTILES
    (x_ref, g_ref, perm_ref, w_ref, cw_ref, cb_ref, wax_ref, bax_ref, lam_ref,
     u_ref, lhs_ref) = refs[:11]
    ext_refs = refs[11:11 + n_sub]
    a_refs = refs[11 + n_sub:11 + 2 * n_sub]
    b_refs = refs[11 + 2 * n_sub:11 + 3 * n_sub]
    carry_ref = refs[11 + 3 * n_sub]
    i = pl.program_id(0)
    c = cw_ref.shape[1]
    n_in = w_ref.shape[1] - c
    n_a = n_in - conv_w
    ua_ref = u_ref.at[:, 0:n_a]
    xr_ref = u_ref.at[:, n_a:n_a + c]
    n_taps = cw_ref.shape[0]
    halo = (n_taps - 1) * SUBLANES
    pz_w = pool_w + conv_w
    v0, lg0 = pool_w, pool_w + 2 * conv_w

    @pl.when(i == 0)
    def _():
        carry_ref[...] = jnp.zeros_like(carry_ref)
        ext_refs[0][0:halo, :] = jnp.zeros((halo, c), F32)

    h = _rms(x_ref[...], g_ref[...]).astype(BF16)
    for sub in range(n_sub):
        r = slice(sub * tm, (sub + 1) * tm)
        lhs_ref[r, :] = jnp.dot(perm_ref[...], h[r, :], preferred_element_type=F32).astype(BF16)
    xr = jnp.dot(lhs_ref[...], w_ref[:, n_in:], preferred_element_type=F32)
    xr_ref[...] = xr
    ua_ref[:, pz_w:] = jax.nn.gelu(jnp.dot(lhs_ref[...], w_ref[:, lg0:n_in], preferred_element_type=F32))
    vg = jnp.dot(lhs_ref[...], w_ref[:, v0:lg0], preferred_element_type=F32)
    ua_ref[:, pool_w:pz_w] = vg[:, :conv_w] * jax.nn.sigmoid(vg[:, conv_w:])
    ua_ref[:, 0:pool_w] = jnp.dot(lhs_ref[...], w_ref[:, 0:pool_w], preferred_element_type=F32)

    offsets = tuple((n_taps - 1 - k) * SUBLANES for k in range(n_taps))
    for sub in reversed(range(n_sub)):
        r0 = sub * tm
        ext_ref = ext_refs[sub]
        after_ref = ext_refs[(sub + 1) % n_sub]
        for j in range(n_taps - 1):
            rows = slice(j * SUBLANES, (j + 1) * SUBLANES)
            _wrap_rows(ext_ref, tm + j * SUBLANES, xr[r0 + j * SUBLANES:r0 + (j + 1) * SUBLANES, :],
                       after_ref[rows, :], forward=True)
        ext_ref[0:tm, :] = xr[r0:r0 + tm, :]
        _lru_coeffs(ext_ref, offsets, cw_ref, cb_ref, wax_ref, bax_ref, lam_ref, a_refs[sub], b_refs[sub])
        _lru_scan(a_refs[sub], b_refs[sub], u_ref.at[r0:r0 + tm, n_a + c:], carry_ref, reverse=True)


def _proj_bwd(x, norm_g, perm, w_in_bf16, cw, cb, wax, bax, lam, layer, *, tm, pool_w, conv_w):
    s, d = x.shape
    d_in = w_in_bf16.shape[-1]
    n_taps = cw.shape[2]
    n_heads, hd = wax.shape[2], wax.shape[3]
    c = n_heads * hd
    na = d_in - c - conv_w
    n_sub = PROJ_SUBTILES
    rows_blk = n_sub * tm
    assert s % rows_blk == 0
    nb = s // rows_blk
    halo = (n_taps - 1) * SUBLANES
    rev = lambda i: (nb - 1 - i, 0)
    body = functools.partial(_proj_bwd_body, pool_w=pool_w, conv_w=conv_w, tm=tm)
    return pl.pallas_call(
        body,
        grid=(nb,),
        in_specs=[
            pl.BlockSpec((rows_blk, d), rev),
            _layer_block((1, d), layer),
            pl.BlockSpec((tm, tm), lambda i: (0, 0), pipeline_mode=pl.Buffered(1)),
            _layer_block((d, d_in), layer),
            _dir_block((n_taps, c), layer, 1),
            _dir_block((1, c), layer, 1),
            _dir_block((n_heads, hd, 2 * hd), layer, 1),
            _dir_block((n_heads, 1, 2 * hd), layer, 1),
            _dir_block((1, c), layer, 1),
        ],
        out_specs=pl.BlockSpec((rows_blk, na + 2 * c), rev),
        out_shape=jax.ShapeDtypeStruct((s, na + 2 * c), F32),
        scratch_shapes=([pltpu.VMEM((rows_blk, d), BF16)]
                        + [pltpu.VMEM((tm + halo, c), F32)] * n_sub
                        + [pltpu.VMEM((tm, c), F32)] * (2 * n_sub)
                        + [pltpu.VMEM((SUBLANES, c), F32)]),
        compiler_params=pltpu.CompilerParams(
            dimension_semantics=("arbitrary",), vmem_limit_bytes=V7X_VMEM_LIMIT_BYTES),
        name="proj_bwd",
    )(x, norm_g, perm, w_in_bf16, cw, cb, wax, bax, lam)


def _mix_step(y_prev_ref, y_ref, x_ref, u_ref, unext_ref,
              pw_ref, ps_ref, dww_ref, dwb_ref, lng_ref, lnb_ref,
              cw_ref, cb_ref, wax_ref, bax_ref, lam_ref, wout_ref, unperm_ref, o_ref,
              pzext_ref, pztail_ref, xext_ref, xrtail_ref, a_ref, b_ref, hf_ref, yp_ref, carry_ref,
              *, seq_len, n_tiles):
    step = pl.program_id(0)
    i = jnp.minimum(step, n_tiles - 1)
    tm = u_ref.shape[0]
    chunk = tm // SUBLANES
    first = i == 0
    last = i == n_tiles - 1
    n_groups, pg = pw_ref.shape[0], pw_ref.shape[1]
    pool_w = n_groups * pg
    conv_w = dww_ref.shape[1]
    pz_w = pool_w + conv_w
    c = cw_ref.shape[1]
    n_a = u_ref.shape[1] - 2 * c
    ua_ref = u_ref.at[:, 0:n_a]
    xr_ref = u_ref.at[:, n_a:n_a + c]
    hb_ref = u_ref.at[:, n_a + c:]
    hv = HALO_GROUPS
    rows = lambda j: slice(j * SUBLANES, (j + 1) * SUBLANES)

    o_ref[...] = x_ref[...] + jnp.dot(y_prev_ref[...], wout_ref[...], preferred_element_type=F32)

    for j in range(hv):
        main_tail = ua_ref[rows(chunk - hv + j), 0:pz_w]
        before = jnp.where(first, 0.0, pztail_ref[rows(j), :])
        _wrap_rows(pzext_ref, j * SUBLANES, main_tail, before, forward=False)
        pztail_ref[rows(j), :] = main_tail
        after = jnp.where(last, 0.0, unext_ref[rows(j), :])
        _wrap_rows(pzext_ref, (hv + chunk + j) * SUBLANES, ua_ref[rows(j), 0:pz_w], after, forward=True)
    pzext_ref[hv * SUBLANES:hv * SUBLANES + tm, :] = ua_ref[:, 0:pz_w]

    pos = lax.broadcasted_iota(jnp.int32, (tm, pg), 0)
    t = i * tm + (pos & (SUBLANES - 1)) * chunk + (pos >> (SUBLANES.bit_length() - 1))
    for g, win in enumerate(POOL_WINDOWS):
        sl = slice(g * pg, (g + 1) * pg)
        half = win // 2
        ssum = pzext_ref[pl.ds((hv - half) * SUBLANES, tm), sl]
        for o in range(-half + 1, half):
            ssum = ssum + pzext_ref[pl.ds((hv + o) * SUBLANES, tm), sl]
        cnt = (jnp.minimum(t + half, seq_len) - jnp.maximum(t - half, 0)).astype(F32)
        pooled = ssum / cnt - ua_ref[:, sl]
        mixed = jnp.dot(pooled.astype(BF16), pw_ref[g], preferred_element_type=F32)
        yp_ref[:, sl] = (mixed * ps_ref[:, sl]).astype(BF16)

    n_taps = dww_ref.shape[0]
    pad = n_taps // 2
    z = dwb_ref[...]
    for k in range(n_taps):
        z = z + dww_ref[k:k + 1, :] * pzext_ref[pl.ds((hv - pad + k) * SUBLANES, tm), pool_w:pz_w]
    mu = jnp.mean(z, axis=-1, keepdims=True)
    zc = z - mu
    var = jnp.mean(zc * zc, axis=-1, keepdims=True)
    zn = zc * lax.rsqrt(var + LN_EPS) * lng_ref[...] + lnb_ref[...]
    yp_ref[:, pool_w:pz_w] = (zn * jax.nn.sigmoid(zn)).astype(BF16)

    lru_taps = cw_ref.shape[0]
    lh = lru_taps - 1
    for j in range(lh):
        main_tail = xr_ref[rows(chunk - lh + j), :]
        before = jnp.where(first, 0.0, xrtail_ref[rows(j), :])
        _wrap_rows(xext_ref, j * SUBLANES, main_tail, before, forward=False)
        xrtail_ref[rows(j), :] = main_tail
    xext_ref[lh * SUBLANES:, :] = xr_ref[...]
    offsets = tuple(k * SUBLANES for k in range(lru_taps))
    _lru_coeffs(xext_ref, offsets, cw_ref, cb_ref, wax_ref, bax_ref, lam_ref, a_ref, b_ref)
    _lru_scan(a_ref, b_ref, hf_ref, carry_ref, reverse=False)
    yp_ref[:, pz_w:] = ((hf_ref[...] + hb_ref[...]) * ua_ref[:, pz_w:]).astype(BF16)

    y_ref[...] = jnp.dot(unperm_ref[...], yp_ref[...], preferred_element_type=F32).astype(BF16)


def _mix_body(*refs, seq_len, n_tiles):
    (x_ref, u_ref, unext_ref,
     pw_ref, ps_ref, dww_ref, dwb_ref, lng_ref, lnb_ref,
     cw_ref, cb_ref, wax_ref, bax_ref, lam_ref, wout_ref, unperm_ref, o_ref,
     pzext_ref, pztail_ref, xext_ref, xrtail_ref, a_ref, b_ref, hf_ref, yp_ref, y_ref,
     carry_ref) = refs
    step = pl.program_id(0)

    @pl.when(step == 0)
    def _():
        carry_ref[...] = jnp.zeros_like(carry_ref)
        y_ref[1] = jnp.zeros(y_ref.shape[1:], y_ref.dtype)
        pztail_ref[...] = jnp.zeros_like(pztail_ref)
        xrtail_ref[...] = jnp.zeros_like(xrtail_ref)

    run = functools.partial(
        _mix_step, x_ref=x_ref, u_ref=u_ref, unext_ref=unext_ref, pw_ref=pw_ref, ps_ref=ps_ref,
        dww_ref=dww_ref, dwb_ref=dwb_ref, lng_ref=lng_ref, lnb_ref=lnb_ref, cw_ref=cw_ref, cb_ref=cb_ref,
        wax_ref=wax_ref, bax_ref=bax_ref, lam_ref=lam_ref, wout_ref=wout_ref, unperm_ref=unperm_ref,
        o_ref=o_ref, pzext_ref=pzext_ref, pztail_ref=pztail_ref, xext_ref=xext_ref, xrtail_ref=xrtail_ref,
        a_ref=a_ref, b_ref=b_ref, hf_ref=hf_ref, yp_ref=yp_ref, carry_ref=carry_ref,
        seq_len=seq_len, n_tiles=n_tiles)

    slot = step % 2
    run(y_ref.at[1 - slot], y_ref.at[slot])


def _mix(x, u, unperm, pool_w, pool_scale, dw_w, dw_b, ln_g, ln_b, cw, cb, wax, bax, lam,
         w_out_bf16, layer, *, tm):
    s, d = x.shape
    nt = s // tm
    n_pool_groups, pg = pool_w.shape[1], pool_w.shape[2]
    pool_width = n_pool_groups * pg
    conv_width = dw_w.shape[2]
    conv_taps = dw_w.shape[1]
    lru_taps = cw.shape[2]
    n_heads, hd = wax.shape[2], wax.shape[3]
    c = n_heads * hd
    d_mix = w_out_bf16.shape[1]
    n_u = u.shape[1]
    pz_w = pool_width + conv_width
    halo_rows = HALO_GROUPS * SUBLANES
    assert conv_taps // 2 <= HALO_GROUPS and max(POOL_WINDOWS) // 2 <= HALO_GROUPS
    assert pz_w + c == d_mix == n_u - 2 * c
    assert tm % halo_rows == 0 and pz_w % 128 == 0 and n_u % pz_w == 0
    halo_per_tile = tm // halo_rows
    last_halo = s // halo_rows - 1

    tile = lambda st: jnp.minimum(st, nt - 1)
    cur = lambda st: (tile(st), 0)
    prev = lambda st: (jnp.maximum(st - 1, 0), 0)
    body = functools.partial(_mix_body, seq_len=s, n_tiles=nt)
    return pl.pallas_call(
        body,
        grid=(nt + 1,),
        in_specs=[
            pl.BlockSpec((tm, d), prev),
            pl.BlockSpec((tm, n_u), cur),
            pl.BlockSpec((halo_rows, pz_w),
                         lambda st: (jnp.minimum((tile(st) + 1) * halo_per_tile, last_halo), 0)),
            _layer_block((n_pool_groups, pg, pg), layer),
            _layer_block((1, pool_width), layer),
            _layer_block((conv_taps, conv_width), layer),
            _layer_block((1, conv_width), layer),
            _layer_block((1, conv_width), layer),
            _layer_block((1, conv_width), layer),
            _dir_block((lru_taps, c), layer, 0),
            _dir_block((1, c), layer, 0),
            _dir_block((n_heads, hd, 2 * hd), layer, 0),
            _dir_block((n_heads, 1, 2 * hd), layer, 0),
            _dir_block((1, c), layer, 0),
            _layer_block((d_mix, d), layer),
            pl.BlockSpec((tm, tm), lambda st: (0, 0), pipeline_mode=pl.Buffered(1)),
        ],
        out_specs=pl.BlockSpec((tm, d), prev),
        out_shape=jax.ShapeDtypeStruct((s, d), F32),
        scratch_shapes=[
            pltpu.VMEM((tm + 2 * halo_rows, pz_w), F32),
            pltpu.VMEM((halo_rows, pz_w), F32),
            pltpu.VMEM((tm + (lru_taps - 1) * SUBLANES, c), F32),
            pltpu.VMEM(((lru_taps - 1) * SUBLANES, c), F32),
            pltpu.VMEM((tm, c), F32),
            pltpu.VMEM((tm, c), F32),
            pltpu.VMEM((tm, c), F32),
            pltpu.VMEM((tm, d_mix), BF16),
            pltpu.VMEM((2, tm, d_mix), BF16),
            pltpu.VMEM((SUBLANES, c), F32),
        ],
        compiler_params=pltpu.CompilerParams(
            dimension_semantics=("arbitrary",), vmem_limit_bytes=V7X_VMEM_LIMIT_BYTES),
        name="mix",
    )(x, u, u, pool_w, pool_scale, dw_w, dw_b, ln_g, ln_b, cw, cb, wax, bax, lam, w_out_bf16, unperm)


def _chunk_permutation(tm):
    chunk = tm // SUBLANES
    pos = jnp.arange(tm)
    src = (pos % SUBLANES) * chunk + pos // SUBLANES
    return (src[:, None] == jnp.arange(tm)[None, :]).astype(BF16)


def kernel(x, norm_ffn1, ffn1_w_gate, ffn1_w_up, ffn1_w_down, norm_mix, w_in, pool_w, pool_scale, conv_dw_w, conv_dw_b, conv_ln_g, conv_ln_b, lru_conv_w, lru_conv_b, lru_w_a, lru_b_a, lru_w_x, lru_b_x, lru_lambda, w_out, norm_ffn2, ffn2_w_gate, ffn2_w_up, ffn2_w_down, norm_final):
    b, s, d = x.shape
    depth = w_in.shape[0]
    n_dirs, n_heads, hd = lru_w_a.shape[1], lru_w_a.shape[2], lru_w_a.shape[3]

    row = lambda p: p[:, None, :]
    norm_ffn1_r, norm_mix_r, norm_ffn2_r = row(norm_ffn1), row(norm_mix), row(norm_ffn2)
    norm_final_r = norm_final[None, :]
    pool_scale_r, dw_b_r, ln_g_r, ln_b_r = row(pool_scale), row(conv_dw_b), row(conv_ln_g), row(conv_ln_b)
    lru_cb_r = lru_conv_b[:, :, None, :]
    lam_r = lru_lambda[:, :, None, :]
    wax = jnp.concatenate([lru_w_a, lru_w_x], axis=-1).astype(BF16)
    bax = jnp.concatenate([lru_b_a.reshape(depth, n_dirs, n_heads, 1, hd),
                           lru_b_x.reshape(depth, n_dirs, n_heads, 1, hd)], axis=-1)
    w_in_b = w_in.astype(BF16)
    w_out_b = w_out.astype(BF16)
    pool_w_b = pool_w.astype(BF16)
    tm_mix = min(MIX_ROWS, s)
    perm = _chunk_permutation(tm_mix)
    unperm = perm.T
    pool_width = pool_w.shape[1] * pool_w.shape[2]
    conv_width = conv_dw_w.shape[2]

    outs = []
    for bi in range(b):
        xb = x[bi]
        for l in range(depth):
            xb = _ffn(xb, norm_ffn1_r, ffn1_w_gate, ffn1_w_up, ffn1_w_down, norm_final_r, l,
                      final_norm=False, tm=FFN_ROWS, tf=FFN_COLS)
            u = _proj_bwd(xb, norm_mix_r, perm, w_in_b, lru_conv_w, lru_cb_r, wax, bax, lam_r, l,
                          tm=tm_mix, pool_w=pool_width, conv_w=conv_width)
            xb = _mix(xb, u, unperm, pool_w_b, pool_scale_r, conv_dw_w, dw_b_r, ln_g_r, ln_b_r,
                      lru_conv_w, lru_cb_r, wax, bax, lam_r, w_out_b, l, tm=tm_mix)
            xb = _ffn(xb, norm_ffn2_r, ffn2_w_gate, ffn2_w_up, ffn2_w_down, norm_final_r, l,
                      final_norm=(l == depth - 1), tm=FFN_ROWS, tf=FFN_COLS)
        outs.append(xb)
    return jnp.stack(outs, axis=0) if b > 1 else outs[0][None]
```

```python
import functools

import jax
import jax.numpy as jnp
from jax import lax
from jax.experimental import pallas as pl
from jax.experimental.pallas import tpu as pltpu

RMS_EPS = 1e-6
LN_EPS = 1e-5
LRU_C = 8.0
POOL_WINDOWS = (2, 4, 8, 16)

SUBLANES = 8
HALO_GROUPS = 16
V7X_VMEM_LIMIT_BYTES = 56 * 1024 * 1024

F32 = jnp.float32
BF16 = jnp.bfloat16

FFN_ROWS = 1024
FFN_COLS = 512
MIX_ROWS = 256
PROJ_SUBTILES = 1


def _rms(x, g):
    return x * lax.rsqrt(jnp.mean(x * x, axis=-1, keepdims=True) + RMS_EPS) * g


def _layer_block(shape, layer):
    zeros = (0,) * len(shape)
    return pl.BlockSpec((None,) + tuple(shape), lambda *_: (layer,) + zeros, pipeline_mode=pl.Buffered(1))


def _dir_block(shape, layer, direction):
    zeros = (0,) * len(shape)
    return pl.BlockSpec((None, None) + tuple(shape), lambda *_: (layer, direction) + zeros,
                        pipeline_mode=pl.Buffered(1))


def _ffn_body(x_hbm, g_ref, wg_ref, wu_ref, wd_ref, gf_ref, o_hbm, acc_ref, h_ref, in_sem, out_sem,
              *, n_ff, d_chunk, final_norm):
    i = pl.program_id(0)
    j = pl.program_id(1)
    n_tiles = pl.num_programs(0)
    tm = acc_ref.shape[1]
    slot = i % 2
    other = 1 - slot

    def x_copy(tile, s):
        return pltpu.make_async_copy(x_hbm.at[pl.ds(tile * tm, tm)], acc_ref.at[s], in_sem.at[s])

    def o_copy(tile, s):
        return pltpu.make_async_copy(acc_ref.at[s], o_hbm.at[pl.ds(tile * tm, tm)], out_sem.at[s])

    @pl.when(j == 0)
    def _():
        @pl.when(i == 0)
        def _():
            x_copy(0, 0).start()

        x_copy(i, slot).wait()
        h_ref[...] = _rms(acc_ref[slot], g_ref[...]).astype(BF16)

    @pl.when(j == min(2, n_ff - 1))
    def _():
        @pl.when(i >= 1)
        def _():
            o_copy(i - 1, other).wait()

        @pl.when(i + 1 < n_tiles)
        def _():
            x_copy(i + 1, other).start()

    h = h_ref[...]
    gate = jnp.dot(h, wg_ref[...].astype(BF16), preferred_element_type=F32)
    up = jnp.dot(h, wu_ref[...].astype(BF16), preferred_element_type=F32)
    act = ((gate * jax.nn.sigmoid(gate)) * (0.5 * up)).astype(BF16)
    d = acc_ref.shape[2]
    for c in range(0, d, d_chunk):
        acc_ref[slot, :, c:c + d_chunk] += jnp.dot(act, wd_ref[:, c:c + d_chunk].astype(BF16),
                                                   preferred_element_type=F32)

    @pl.when(j == n_ff - 1)
    def _():
        if final_norm:
            acc_ref[slot] = _rms(acc_ref[slot], gf_ref[...])
        o_copy(i, slot).start()

        @pl.when(i == n_tiles - 1)
        def _():
            o_copy(i, slot).wait()


def _ffn(x, norm_g, w_gate, w_up, w_down, norm_final, layer, *, final_norm, tm, tf):
    s, d = x.shape
    f = w_gate.shape[-1]
    tm = min(tm, s)
    tf = min(tf, f)
    n_ff = f // tf
    assert n_ff >= 2
    body = functools.partial(_ffn_body, n_ff=n_ff, d_chunk=min(512, d), final_norm=final_norm)
    return pl.pallas_call(
        body,
        grid=(s // tm, n_ff),
        in_specs=[
            pl.BlockSpec(memory_space=pl.ANY),
            pl.BlockSpec((None, 1, d), lambda i, j: (layer, 0, 0)),
            pl.BlockSpec((None, d, tf), lambda i, j: (layer, 0, j)),
            pl.BlockSpec((None, d, tf), lambda i, j: (layer, 0, j)),
            pl.BlockSpec((None, tf, d), lambda i, j: (layer, j, 0)),
            pl.BlockSpec((1, d), lambda i, j: (0, 0)),
        ],
        out_specs=pl.BlockSpec(memory_space=pl.ANY),
        out_shape=jax.ShapeDtypeStruct((s, d), F32),
        scratch_shapes=[pltpu.VMEM((2, tm, d), F32), pltpu.VMEM((tm, d), BF16),
                        pltpu.SemaphoreType.DMA((2,)), pltpu.SemaphoreType.DMA((2,))],
        compiler_params=pltpu.CompilerParams(
            dimension_semantics=("arbitrary", "arbitrary"),
            vmem_limit_bytes=V7X_VMEM_LIMIT_BYTES),
        name="ffn",
    )(x, norm_g, w_gate, w_up, w_down, norm_final)


def _lru_coeffs(ext_ref, offsets, cw_ref, cb_ref, wax_ref, bax_ref, lam_ref, a_ref, b_ref):
    tm, c = a_ref.shape
    n_heads = wax_ref.shape[0]
    hd = c // n_heads
    softplus_neg_lam = jax.nn.softplus(-lam_ref[...])
    for h in range(n_heads):
        sl = slice(h * hd, (h + 1) * hd)
        xh = cb_ref[:, sl] + cw_ref[0:1, sl] * ext_ref[pl.ds(offsets[0], tm), sl]
        for k in range(1, len(offsets)):
            xh = xh + cw_ref[k:k + 1, sl] * ext_ref[pl.ds(offsets[k], tm), sl]
        ri = jnp.dot(xh.astype(BF16), wax_ref[h], preferred_element_type=F32) + bax_ref[h]
        r = jax.nn.sigmoid(ri[:, :hd])
        ig = jax.nn.sigmoid(ri[:, hd:])
        log_a = -LRU_C * r * softplus_neg_lam[:, sl]
        a = jnp.exp(log_a)
        a_ref[:, sl] = a
        v = 1.0 - a * a
        b_ref[:, sl] = jnp.where(v > 0.0, v * lax.rsqrt(v), 0.0) * (ig * xh)


def _sublane_scan(a, b, *, reverse):
    row = lax.broadcasted_iota(jnp.int32, a.shape, 0)
    for dist in (1, 2, 4):
        if reverse:
            keep = row < SUBLANES - dist
            shift = SUBLANES - dist
        else:
            keep = row >= dist
            shift = dist
        a_n = jnp.where(keep, pltpu.roll(a, shift, 0), 1.0)
        b_n = jnp.where(keep, pltpu.roll(b, shift, 0), 0.0)
        b = a * b_n + b
        a = a * a_n
    return a, b


def _lru_scan(a_ref, b_ref, h_ref, carry_ref, *, reverse):
    tm, c = a_ref.shape
    n_groups = tm // SUBLANES
    order = range(n_groups - 1, -1, -1) if reverse else range(n_groups)
    rows = lambda j: slice(j * SUBLANES, (j + 1) * SUBLANES)
    row = lax.broadcasted_iota(jnp.int32, (SUBLANES, c), 0)

    h = jnp.zeros((SUBLANES, c), F32)
    p = jnp.ones((SUBLANES, c), F32)
    for j in order:
        a = a_ref[rows(j), :]
        h = a * h + b_ref[rows(j), :]
        p = a * p
    a_cum, b_cum = _sublane_scan(p, h, reverse=reverse)
    carry = carry_ref[...]
    h_edge = a_cum * carry + b_cum
    if reverse:
        h = jnp.where(row == SUBLANES - 1, carry, pltpu.roll(h_edge, SUBLANES - 1, 0))
        carry_ref[...] = jnp.broadcast_to(h_edge[0:1, :], (SUBLANES, c))
    else:
        h = jnp.where(row == 0, carry, pltpu.roll(h_edge, 1, 0))
        carry_ref[...] = jnp.broadcast_to(h_edge[SUBLANES - 1:SUBLANES, :], (SUBLANES, c))
    for j in order:
        h = a_ref[rows(j), :] * h + b_ref[rows(j), :]
        h_ref[rows(j), :] = h


def _wrap_rows(dst_ref, dst_row, inner, outer, *, forward):
    row = lax.broadcasted_iota(jnp.int32, inner.shape, 0)
    if forward:
        val = jnp.where(row == SUBLANES - 1, pltpu.roll(outer, SUBLANES - 1, 0),
                        pltpu.roll(inner, SUBLANES - 1, 0))
    else:
        val = jnp.where(row == 0, pltpu.roll(outer, 1, 0), pltpu.roll(inner, 1, 0))
    dst_ref[dst_row:dst_row + SUBLANES, :] = val


def _proj_bwd_body(*refs, pool_w, conv_w, tm):
    n_sub = PROJ_SUBTILES
    (x_ref, g_ref, perm_ref, w_ref, cw_ref, cb_ref, wax_ref, bax_ref, lam_ref,
     u_ref, lhs_ref) = refs[:11]
    ext_refs = refs[11:11 + n_sub]
    a_refs = refs[11 + n_sub:11 + 2 * n_sub]
    b_refs = refs[11 + 2 * n_sub:11 + 3 * n_sub]
    carry_ref = refs[11 + 3 * n_sub]
    i = pl.program_id(0)
    c = cw_ref.shape[1]
    n_in = w_ref.shape[1] - c
    n_a = n_in - conv_w
    ua_ref = u_ref.at[:, 0:n_a]
    xr_ref = u_ref.at[:, n_a:n_a + c]
    n_taps = cw_ref.shape[0]
    halo = (n_taps - 1) * SUBLANES
    pz_w = pool_w + conv_w
    v0, lg0 = pool_w, pool_w + 2 * conv_w

    @pl.when(i == 0)
    def _():
        carry_ref[...] = jnp.zeros_like(carry_ref)
        ext_refs[0][0:halo, :] = jnp.zeros((halo, c), F32)

    h = _rms(x_ref[...], g_ref[...]).astype(BF16)
    for sub in range(n_sub):
        r = slice(sub * tm, (sub + 1) * tm)
        lhs_ref[r, :] = jnp.dot(perm_ref[...], h[r, :], preferred_element_type=F32).astype(BF16)
    xr = jnp.dot(lhs_ref[...], w_ref[:, n_in:], preferred_element_type=F32)
    xr_ref[...] = xr
    ua_ref[:, pz_w:] = jax.nn.gelu(jnp.dot(lhs_ref[...], w_ref[:, lg0:n_in], preferred_element_type=F32))
    vg = jnp.dot(lhs_ref[...], w_ref[:, v0:lg0], preferred_element_type=F32)
    ua_ref[:, pool_w:pz_w] = vg[:, :conv_w] * jax.nn.sigmoid(vg[:, conv_w:])
    ua_ref[:, 0:pool_w] = jnp.dot(lhs_ref[...], w_ref[:, 0:pool_w], preferred_element_type=F32)

    offsets = tuple((n_taps - 1 - k) * SUBLANES for k in range(n_taps))
    for sub in reversed(range(n_sub)):
        r0 = sub * tm
        ext_ref = ext_refs[sub]
        after_ref = ext_refs[(sub + 1) % n_sub]
        for j in range(n_taps - 1):
            rows = slice(j * SUBLANES, (j + 1) * SUBLANES)
            _wrap_rows(ext_ref, tm + j * SUBLANES, xr[r0 + j * SUBLANES:r0 + (j + 1) * SUBLANES, :],
                       after_ref[rows, :], forward=True)
        ext_ref[0:tm, :] = xr[r0:r0 + tm, :]
        _lru_coeffs(ext_ref, offsets, cw_ref, cb_ref, wax_ref, bax_ref, lam_ref, a_refs[sub], b_refs[sub])
        _lru_scan(a_refs[sub], b_refs[sub], u_ref.at[r0:r0 + tm, n_a + c:], carry_ref, reverse=True)


def _proj_bwd(x, norm_g, perm, w_in_bf16, cw, cb, wax, bax, lam, layer, *, tm, pool_w, conv_w):
    s, d = x.shape
    d_in = w_in_bf16.shape[-1]
    n_taps = cw.shape[2]
    n_heads, hd = wax.shape[2], wax.shape[3]
    c = n_heads * hd
    na = d_in - c - conv_w
    n_sub = PROJ_SUBTILES
    rows_blk = n_sub * tm
    assert s % rows_blk == 0
    nb = s // rows_blk
    halo = (n_taps - 1) * SUBLANES
    rev = lambda i: (nb - 1 - i, 0)
    body = functools.partial(_proj_bwd_body, pool_w=pool_w, conv_w=conv_w, tm=tm)
    return pl.pallas_call(
        body,
        grid=(nb,),
        in_specs=[
            pl.BlockSpec((rows_blk, d), rev),
            _layer_block((1, d), layer),
            pl.BlockSpec((tm, tm), lambda i: (0, 0), pipeline_mode=pl.Buffered(1)),
            _layer_block((d, d_in), layer),
            _dir_block((n_taps, c), layer, 1),
            _dir_block((1, c), layer, 1),
            _dir_block((n_heads, hd, 2 * hd), layer, 1),
            _dir_block((n_heads, 1, 2 * hd), layer, 1),
            _dir_block((1, c), layer, 1),
        ],
        out_specs=pl.BlockSpec((rows_blk, na + 2 * c), rev),
        out_shape=jax.ShapeDtypeStruct((s, na + 2 * c), F32),
        scratch_shapes=([pltpu.VMEM((rows_blk, d), BF16)]
                        + [pltpu.VMEM((tm + halo, c), F32)] * n_sub
                        + [pltpu.VMEM((tm, c), F32)] * (2 * n_sub)
                        + [pltpu.VMEM((SUBLANES, c), F32)]),
        compiler_params=pltpu.CompilerParams(
            dimension_semantics=("arbitrary",), vmem_limit_bytes=V7X_VMEM_LIMIT_BYTES),
        name="proj_bwd",
    )(x, norm_g, perm, w_in_bf16, cw, cb, wax, bax, lam)


def _mix_step(y_prev_ref, y_ref, x_ref, u_ref, unext_ref,
              pw_ref, ps_ref, dww_ref, dwb_ref, lng_ref, lnb_ref,
              cw_ref, cb_ref, wax_ref, bax_ref, lam_ref, wout_ref, unperm_ref, o_ref,
              pzext_ref, pztail_ref, xext_ref, xrtail_ref, a_ref, b_ref, hf_ref, yp_ref, carry_ref,
              *, seq_len, n_tiles):
    step = pl.program_id(0)
    i = jnp.minimum(step, n_tiles - 1)
    tm = u_ref.shape[0]
    chunk = tm // SUBLANES
    first = i == 0
    last = i == n_tiles - 1
    n_groups, pg = pw_ref.shape[0], pw_ref.shape[1]
    pool_w = n_groups * pg
    conv_w = dww_ref.shape[1]
    pz_w = pool_w + conv_w
    c = cw_ref.shape[1]
    n_a = u_ref.shape[1] - 2 * c
    ua_ref = u_ref.at[:, 0:n_a]
    xr_ref = u_ref.at[:, n_a:n_a + c]
    hb_ref = u_ref.at[:, n_a + c:]
    hv = HALO_GROUPS
    rows = lambda j: slice(j * SUBLANES, (j + 1) * SUBLANES)

    o_ref[...] = x_ref[...] + jnp.dot(y_prev_ref[...], wout_ref[...], preferred_element_type=F32)

    for j in range(hv):
        main_tail = ua_ref[rows(chunk - hv + j), 0:pz_w]
        before = jnp.where(first, 0.0, pztail_ref[rows(j), :])
        _wrap_rows(pzext_ref, j * SUBLANES, main_tail, before, forward=False)
        pztail_ref[rows(j), :] = main_tail
        after = jnp.where(last, 0.0, unext_ref[rows(j), :])
        _wrap_rows(pzext_ref, (hv + chunk + j) * SUBLANES, ua_ref[rows(j), 0:pz_w], after, forward=True)
    pzext_ref[hv * SUBLANES:hv * SUBLANES + tm, :] = ua_ref[:, 0:pz_w]

    pos = lax.broadcasted_iota(jnp.int32, (tm, pg), 0)
    t = i * tm + (pos & (SUBLANES - 1)) * chunk + (pos >> 3)
    for g, win in enumerate(POOL_WINDOWS):
        sl = slice(g * pg, (g + 1) * pg)
        half = win // 2
        ssum = pzext_ref[pl.ds((hv - half) * SUBLANES, tm), sl]
        for o in range(-half + 1, half):
            ssum = ssum + pzext_ref[pl.ds((hv + o) * SUBLANES, tm), sl]
        cnt = (jnp.minimum(t + half, seq_len) - jnp.maximum(t - half, 0)).astype(F32)
        pooled = ssum / cnt - ua_ref[:, sl]
        mixed = jnp.dot(pooled.astype(BF16), pw_ref[g], preferred_element_type=F32)
        yp_ref[:, sl] = (mixed * ps_ref[:, sl]).astype(BF16)

    n_taps = dww_ref.shape[0]
    pad = n_taps // 2
    z = dwb_ref[...]
    for k in range(n_taps):
        z = z + dww_ref[k:k + 1, :] * pzext_ref[pl.ds((hv - pad + k) * SUBLANES, tm), pool_w:pz_w]
    mu = jnp.mean(z, axis=-1, keepdims=True)
    zc = z - mu
    var = jnp.mean(zc * zc, axis=-1, keepdims=True)
    zn = zc * lax.rsqrt(var + LN_EPS) * lng_ref[...] + lnb_ref[...]
    yp_ref[:, pool_w:pz_w] = (zn * jax.nn.sigmoid(zn)).astype(BF16)

    lru_taps = cw_ref.shape[0]
    lh = lru_taps - 1
    for j in range(lh):
        main_tail = xr_ref[rows(chunk - lh + j), :]
        before = jnp.where(first, 0.0, xrtail_ref[rows(j), :])
        _wrap_rows(xext_ref, j * SUBLANES, main_tail, before, forward=False)
        xrtail_ref[rows(j), :] = main_tail
    xext_ref[lh * SUBLANES:, :] = xr_ref[...]
    offsets = tuple(k * SUBLANES for k in range(lru_taps))
    _lru_coeffs(xext_ref, offsets, cw_ref, cb_ref, wax_ref, bax_ref, lam_ref, a_ref, b_ref)
    _lru_scan(a_ref, b_ref, hf_ref, carry_ref, reverse=False)
    yp_ref[:, pz_w:] = ((hf_ref[...] + hb_ref[...]) * ua_ref[:, pz_w:]).astype(BF16)

    y_ref[...] = jnp.dot(unperm_ref[...], yp_ref[...], preferred_element_type=F32).astype(BF16)


def _mix_body(*refs, seq_len, n_tiles):
    (x_ref, u_ref, unext_ref,
     pw_ref, ps_ref, dww_ref, dwb_ref, lng_ref, lnb_ref,
     cw_ref, cb_ref, wax_ref, bax_ref, lam_ref, wout_ref, unperm_ref, o_ref,
     pzext_ref, pztail_ref, xext_ref, xrtail_ref, a_ref, b_ref, hf_ref, yp_ref, y_ref,
     carry_ref) = refs
    step = pl.program_id(0)

    @pl.when(step == 0)
    def _():
        carry_ref[...] = jnp.zeros_like(carry_ref)
        y_ref[1] = jnp.zeros(y_ref.shape[1:], y_ref.dtype)
        pztail_ref[...] = jnp.zeros_like(pztail_ref)
        xrtail_ref[...] = jnp.zeros_like(xrtail_ref)

    run = functools.partial(
        _mix_step, x_ref=x_ref, u_ref=u_ref, unext_ref=unext_ref, pw_ref=pw_ref, ps_ref=ps_ref,
        dww_ref=dww_ref, dwb_ref=dwb_ref, lng_ref=lng_ref, lnb_ref=lnb_ref, cw_ref=cw_ref, cb_ref=cb_ref,
        wax_ref=wax_ref, bax_ref=bax_ref, lam_ref=lam_ref, wout_ref=wout_ref, unperm_ref=unperm_ref,
        o_ref=o_ref, pzext_ref=pzext_ref, pztail_ref=pztail_ref, xext_ref=xext_ref, xrtail_ref=xrtail_ref,
        a_ref=a_ref, b_ref=b_ref, hf_ref=hf_ref, yp_ref=yp_ref, carry_ref=carry_ref,
        seq_len=seq_len, n_tiles=n_tiles)

    slot = step % 2
    run(y_ref.at[1 - slot], y_ref.at[slot])


def _mix(x, u, unperm, pool_w, pool_scale, dw_w, dw_b, ln_g, ln_b, cw, cb, wax, bax, lam,
         w_out_bf16, layer, *, tm):
    s, d = x.shape
    nt = s // tm
    n_pool_groups, pg = pool_w.shape[1], pool_w.shape[2]
    pool_width = n_pool_groups * pg
    conv_width = dw_w.shape[2]
    conv_taps = dw_w.shape[1]
    lru_taps = cw.shape[2]
    n_heads, hd = wax.shape[2], wax.shape[3]
    c = n_heads * hd
    d_mix = w_out_bf16.shape[1]
    n_u = u.shape[1]
    pz_w = pool_width + conv_width
    halo_rows = HALO_GROUPS * SUBLANES
    assert conv_taps // 2 <= HALO_GROUPS and max(POOL_WINDOWS) // 2 <= HALO_GROUPS
    assert pz_w + c == d_mix == n_u - 2 * c
    assert tm % halo_rows == 0 and pz_w % 128 == 0 and n_u % pz_w == 0
    halo_per_tile = tm // halo_rows
    last_halo = s // halo_rows - 1

    tile = lambda st: jnp.minimum(st, nt - 1)
    cur = lambda st: (tile(st), 0)
    prev = lambda st: (jnp.maximum(st - 1, 0), 0)
    body = functools.partial(_mix_body, seq_len=s, n_tiles=nt)
    return pl.pallas_call(
        body,
        grid=(nt + 1,),
        in_specs=[
            pl.BlockSpec((tm, d), prev),
            pl.BlockSpec((tm, n_u), cur),
            pl.BlockSpec((halo_rows, pz_w),
                         lambda st: (jnp.minimum((tile(st) + 1) * halo_per_tile, last_halo), 0)),
            _layer_block((n_pool_groups, pg, pg), layer),
            _layer_block((1, pool_width), layer),
            _layer_block((conv_taps, conv_width), layer),
            _layer_block((1, conv_width), layer),
            _layer_block((1, conv_width), layer),
            _layer_block((1, conv_width), layer),
            _dir_block((lru_taps, c), layer, 0),
            _dir_block((1, c), layer, 0),
            _dir_block((n_heads, hd, 2 * hd), layer, 0),
            _dir_block((n_heads, 1, 2 * hd), layer, 0),
            _dir_block((1, c), layer, 0),
            _layer_block((d_mix, d), layer),
            pl.BlockSpec((tm, tm), lambda st: (0, 0), pipeline_mode=pl.Buffered(1)),
        ],
        out_specs=pl.BlockSpec((tm, d), prev),
        out_shape=jax.ShapeDtypeStruct((s, d), F32),
        scratch_shapes=[
            pltpu.VMEM((tm + 2 * halo_rows, pz_w), F32),
            pltpu.VMEM((halo_rows, pz_w), F32),
            pltpu.VMEM((tm + (lru_taps - 1) * SUBLANES, c), F32),
            pltpu.VMEM(((lru_taps - 1) * SUBLANES, c), F32),
            pltpu.VMEM((tm, c), F32),
            pltpu.VMEM((tm, c), F32),
            pltpu.VMEM((tm, c), F32),
            pltpu.VMEM((tm, d_mix), BF16),
            pltpu.VMEM((2, tm, d_mix), BF16),
            pltpu.VMEM((SUBLANES, c), F32),
        ],
        compiler_params=pltpu.CompilerParams(
            dimension_semantics=("arbitrary",), vmem_limit_bytes=V7X_VMEM_LIMIT_BYTES),
        name="mix",
    )(x, u, u, pool_w, pool_scale, dw_w, dw_b, ln_g, ln_b, cw, cb, wax, bax, lam, w_out_bf16, unperm)


def _chunk_permutation(tm):
    chunk = tm // SUBLANES
    pos = jnp.arange(tm)
    src = (pos % SUBLANES) * chunk + pos // SUBLANES
    return (src[:, None] == jnp.arange(tm)[None, :]).astype(BF16)


def kernel(x, norm_ffn1, ffn1_w_gate, ffn1_w_up, ffn1_w_down, norm_mix, w_in, pool_w, pool_scale, conv_dw_w, conv_dw_b, conv_ln_g, conv_ln_b, lru_conv_w, lru_conv_b, lru_w_a, lru_b_a, lru_w_x, lru_b_x, lru_lambda, w_out, norm_ffn2, ffn2_w_gate, ffn2_w_up, ffn2_w_down, norm_final):
    b, s, d = x.shape
    depth = w_in.shape[0]
    n_dirs, n_heads, hd = lru_w_a.shape[1], lru_w_a.shape[2], lru_w_a.shape[3]

    row = lambda p: p[:, None, :]
    norm_ffn1_r, norm_mix_r, norm_ffn2_r = row(norm_ffn1), row(norm_mix), row(norm_ffn2)
    norm_final_r = norm_final[None, :]
    pool_scale_r, dw_b_r, ln_g_r, ln_b_r = row(pool_scale), row(conv_dw_b), row(conv_ln_g), row(conv_ln_b)
    lru_cb_r = lru_conv_b[:, :, None, :]
    lam_r = lru_lambda[:, :, None, :]
    wax = jnp.concatenate([lru_w_a, lru_w_x], axis=-1).astype(BF16)
    bax = jnp.concatenate([lru_b_a.reshape(depth, n_dirs, n_heads, 1, hd),
                           lru_b_x.reshape(depth, n_dirs, n_heads, 1, hd)], axis=-1)
    w_in_b = w_in.astype(BF16)
    w_out_b = w_out.astype(BF16)
    pool_w_b = pool_w.astype(BF16)
    tm_mix = min(MIX_ROWS, s)
    perm = _chunk_permutation(tm_mix)
    unperm = perm.T
    pool_width = pool_w.shape[1] * pool_w.shape[2]
    conv_width = conv_dw_w.shape[2]

    outs = []
    for bi in range(b):
        xb = x[bi]
        for l in range(depth):
            xb = _ffn(xb, norm_ffn1_r, ffn1_w_gate, ffn1_w_up, ffn1_w_down, norm_final_r, l,
                      final_norm=False, tm=FFN_ROWS, tf=FFN_COLS)
            u = _proj_bwd(xb, norm_mix_r, perm, w_in_b, lru_conv_w, lru_cb_r, wax, bax, lam_r, l,
                          tm=tm_mix, pool_w=pool_width, conv_w=conv_width)
            xb = _mix(xb, u, unperm, pool_w_b, pool_scale_r, conv_dw_w, dw_b_r, ln_g_r, ln_b_r,
                      lru_conv_w, lru_cb_r, wax, bax, lam_r, w_out_b, l, tm=tm_mix)
            xb = _ffn(xb, norm_ffn2_r, ffn2_w_gate, ffn2_w_up, ffn2_w_down, norm_final_r, l,
                      final_norm=(l == depth - 1), tm=FFN_ROWS, tf=FFN_COLS)
        outs.append(xb)
    return jnp.stack(outs, axis=0) if b > 1 else outs[0][None]
```
